```python
import jax, jax.numpy as jnp
from jax import lax
import numpy as np

D_MODEL = 4096
BATCH = 1
SEQ = 8192
DEPTH = 4

N_MIXERS = 4
NORM_EPS = 1e-6
LN_EPS = 1e-5
L2_EPS = 1e-6
ROPE_THETA = 10000.0
ATT_HEAD_DIM = 128
ATT_HEADS = D_MODEL // ATT_HEAD_DIM
ATT_KV_HEADS = ATT_HEADS // 4
WINDOW = 128
BLOCK = WINDOW
GDN_HEAD_DIM = 128
GDN_QK_HEADS = D_MODEL // (2 * GDN_HEAD_DIM)
GDN_V_HEADS = D_MODEL // GDN_HEAD_DIM
GDN_KEY_DIM = GDN_QK_HEADS * GDN_HEAD_DIM
GDN_VAL_DIM = GDN_V_HEADS * GDN_HEAD_DIM
GDN_CONV = 4
GDN_CHUNK = 64
CONF_KERNEL = 31
SHORT_KERNEL = 3
D_FF = ((8 * D_MODEL // 3 + 255) // 256) * 256
FFN_KERNEL = 3

kernel_name = 'hybrid_swa_gdn_conformer_shortconv_trunk'

F32 = jnp.float32


def _layers_of(kind):
    return len(range(kind, DEPTH, N_MIXERS))


def rms_norm(x, g):
    xf = x.astype(F32)
    y = xf * lax.rsqrt(jnp.mean(xf * xf, axis=-1, keepdims=True) + NORM_EPS)
    return (y * g.astype(F32)).astype(x.dtype)


def layer_norm(x, g, b):
    xf = x.astype(F32)
    xc = xf - jnp.mean(xf, axis=-1, keepdims=True)
    var = jnp.mean(xc * xc, axis=-1, keepdims=True)
    return (xc * lax.rsqrt(var + LN_EPS) * g.astype(F32) + b.astype(F32)).astype(x.dtype)


def l2_norm(x):
    xf = x.astype(F32)
    return xf * lax.rsqrt(jnp.sum(xf * xf, axis=-1, keepdims=True) + L2_EPS)


def causal_dwconv(x, w):
    K = w.shape[0]
    S = x.shape[1]
    xp = jnp.pad(x, ((0, 0), (K - 1, 0), (0, 0)))
    y = xp[:, 0:S] * w[0]
    for k in range(1, K):
        y = y + xp[:, k:k + S] * w[k]
    return y


def rope(x, pos):
    half = x.shape[-1] // 2
    inv = jnp.power(ROPE_THETA, -jnp.arange(half, dtype=F32) / half)
    ang = pos.astype(F32)[:, None] * inv[None, :]
    cos = jnp.cos(ang)[None, :, None, :]
    sin = jnp.sin(ang)[None, :, None, :]
    xf = x.astype(F32)
    x1, x2 = xf[..., :half], xf[..., half:]
    return jnp.concatenate([x1 * cos - x2 * sin, x2 * cos + x1 * sin], axis=-1).astype(x.dtype)


def swa_sink_attention(h, w_qkv, w_o, sinks):
    B, S, _ = h.shape
    H, KV, D = ATT_HEADS, ATT_KV_HEADS, ATT_HEAD_DIM
    G = H // KV
    NB = S // BLOCK
    q, k, v = jnp.split(h @ w_qkv, [H * D, H * D + KV * D], axis=-1)
    pos = jnp.arange(S)
    q = rope(q.reshape(B, S, H, D), pos)
    k = rope(k.reshape(B, S, KV, D), pos)
    v = v.reshape(B, S, KV, D)
    qb = q.reshape(B, NB, BLOCK, KV, G, D)

    def band(t):
        tp = jnp.pad(t, ((0, 0), (BLOCK, 0), (0, 0), (0, 0))).reshape(B, NB + 1, BLOCK, KV, D)
        return jnp.concatenate([tp[:, :-1], tp[:, 1:]], axis=2)

    kb, vb = band(k), band(v)
    s = jnp.einsum('bnqhgd,bnkhd->bnhgqk', qb, kb, preferred_element_type=F32) * (D ** -0.5)
    qi = jnp.arange(BLOCK)[:, None]
    kj = jnp.arange(2 * BLOCK)[None, :]
    rel = BLOCK + qi - kj
    in_win = (rel >= 0) & (rel < WINDOW)
    kpos = jnp.arange(NB)[:, None] * BLOCK - BLOCK + kj
    mask = in_win[None, :, :] & (kpos >= 0)[:, None, :]
    s = jnp.where(mask[None, :, None, None, :, :], s, -jnp.inf)
    sink = sinks.astype(F32).reshape(KV, G)[None, None, :, :, None, None]
    m = jnp.maximum(jnp.max(s, axis=-1, keepdims=True), sink)
    p = jnp.exp(s - m)
    denom = jnp.sum(p, axis=-1, keepdims=True) + jnp.exp(sink - m)
    o = jnp.einsum('bnhgqk,bnkhd->bnqhgd', (p / denom).astype(vb.dtype), vb)
    return o.reshape(B, S, H * D) @ w_o


def chunk_gated_delta_rule(q, k, v, g, beta):
    B, S, H, Dk = k.shape
    Dv = v.shape[-1]
    C = GDN_CHUNK
    N = S // C

    def chunks(t):
        return jnp.moveaxis(t.astype(F32).reshape(B, N, C, H, t.shape[-1]), 3, 2)

    qc = chunks(q) * (Dk ** -0.5)
    kc, vc = chunks(k), chunks(v)
    gc = jnp.moveaxis(g.astype(F32).reshape(B, N, C, H), 3, 2)
    bc = jnp.moveaxis(beta.astype(F32).reshape(B, N, C, H), 3, 2)
    gcum = jnp.cumsum(gc, axis=-1)
    tril = jnp.tril(jnp.ones((C, C), bool))
    strict = jnp.tril(jnp.ones((C, C), bool), -1)
    decay = jnp.exp(jnp.where(tril, gcum[..., :, None] - gcum[..., None, :], -jnp.inf))
    kbeta = kc * bc[..., None]
    vbeta = vc * bc[..., None]
    a_kk = jnp.einsum('bnhid,bnhjd->bnhij', kbeta, kc) * decay
    lower = jnp.where(strict, a_kk, 0.0) + jnp.eye(C, dtype=F32)
    rhs = jnp.concatenate([vbeta, kbeta * jnp.exp(gcum)[..., None]], axis=-1)
    sol = lax.linalg.triangular_solve(lower, rhs, left_side=True, lower=True, unit_diagonal=True)
    u, w = sol[..., :Dv], sol[..., Dv:]
    attn = jnp.einsum('bnhid,bnhjd->bnhij', qc, kc) * decay
    qg = qc * jnp.exp(gcum)[..., None]
    glast = gcum[..., -1]
    kd = kc * jnp.exp(glast[..., None] - gcum)[..., None]

    def step(state, xs):
        u_n, w_n, qg_n, kd_n, attn_n, gl_n = xs
        v_new = u_n - jnp.einsum('bhck,bhkv->bhcv', w_n, state)
        out = jnp.einsum('bhck,bhkv->bhcv', qg_n, state) + jnp.einsum('bhij,bhjv->bhiv', attn_n, v_new)
        state = state * jnp.exp(gl_n)[..., None, None] + jnp.einsum('bhck,bhcv->bhkv', kd_n, v_new)
        return state, out

    xs = (jnp.moveaxis(u, 1, 0), jnp.moveaxis(w, 1, 0), jnp.moveaxis(qg, 1, 0),
          jnp.moveaxis(kd, 1, 0), jnp.moveaxis(attn, 1, 0), jnp.moveaxis(glast, 1, 0))
    state0 = jnp.zeros((B, H, Dk, Dv), F32)
    _, out = lax.scan(step, state0, xs)
    return jnp.moveaxis(out, 0, 1).transpose(0, 1, 3, 2, 4).reshape(B, S, H, Dv)


def gated_deltanet(h, w_in, conv_w, a_log, dt_bias, norm_w, w_o):
    B, S, _ = h.shape
    HK, HV, D = GDN_QK_HEADS, GDN_V_HEADS, GDN_HEAD_DIM
    n_qkv = 2 * GDN_KEY_DIM + GDN_VAL_DIM
    qkv, z, b, a = jnp.split(h @ w_in, [n_qkv, n_qkv + GDN_VAL_DIM, n_qkv + GDN_VAL_DIM + HV], axis=-1)
    qkv = jax.nn.silu(causal_dwconv(qkv, conv_w))
    q, k, v = jnp.split(qkv, [GDN_KEY_DIM, 2 * GDN_KEY_DIM], axis=-1)
    rep = HV // HK
    q = jnp.repeat(l2_norm(q.reshape(B, S, HK, D)), rep, axis=2)
    k = jnp.repeat(l2_norm(k.reshape(B, S, HK, D)), rep, axis=2)
    v = v.reshape(B, S, HV, D)
    beta = jax.nn.sigmoid(b.astype(F32))
    g = -jnp.exp(a_log.astype(F32)) * jax.nn.softplus(a.astype(F32) + dt_bias.astype(F32))
    o = chunk_gated_delta_rule(q, k, v, g, beta)
    o = rms_norm(o, norm_w) * jax.nn.silu(z.reshape(B, S, HV, D).astype(F32))
    return o.reshape(B, S, GDN_VAL_DIM).astype(h.dtype) @ w_o


def conformer_conv(h, w_pw1, b_pw1, w_dw, b_dw, ln_g, ln_b, w_pw2, b_pw2):
    val, gate = jnp.split(h @ w_pw1 + b_pw1, 2, axis=-1)
    u = val * jax.nn.sigmoid(gate)
    u = causal_dwconv(u, w_dw) + b_dw
    u = jax.nn.silu(layer_norm(u, ln_g, ln_b))
    return u @ w_pw2 + b_pw2


def short_gated_conv(h, w_in, w_conv, w_out):
    bg, cg, xin = jnp.split(h @ w_in, 3, axis=-1)
    return (bg * causal_dwconv(cg * xin, w_conv)) @ w_out


def conv_glu_ffn(h, w_gate, w_up, w_conv, b_conv, w_down):
    gt = causal_dwconv(h @ w_gate, w_conv) + b_conv
    return (jax.nn.silu(gt) * (h @ w_up)) @ w_down


def setup_inputs(seed: int = 0) -> dict:
    key = jax.random.key(seed)
    kit = iter(list(jax.random.split(key, 48)))

    def nrm(shape, scale):
        return jax.random.normal(next(kit), shape, F32) * scale

    def uni(shape, lo, hi):
        return jax.random.uniform(next(kit), shape, F32, lo, hi)

    nA, nB, nC, nD = (_layers_of(m) for m in range(N_MIXERS))
    Dm = D_MODEL
    att_in = ATT_HEADS * ATT_HEAD_DIM + 2 * ATT_KV_HEADS * ATT_HEAD_DIM
    gdn_in = 2 * GDN_KEY_DIM + 2 * GDN_VAL_DIM + 2 * GDN_V_HEADS
    gdn_conv_ch = 2 * GDN_KEY_DIM + GDN_VAL_DIM
    dt = jnp.exp(uni((nB, GDN_V_HEADS), float(np.log(1e-3)), float(np.log(1e-1))))
    return {
        'x': nrm((BATCH, SEQ, Dm), 1.0),
        'mix_norm': 1.0 + nrm((DEPTH, Dm), 0.02),
        'ffn_norm': 1.0 + nrm((DEPTH, Dm), 0.02),
        'final_norm': 1.0 + nrm((Dm,), 0.02),
        'a_w_qkv': nrm((nA, Dm, att_in), Dm ** -0.5),
        'a_w_o': nrm((nA, ATT_HEADS * ATT_HEAD_DIM, Dm), (ATT_HEADS * ATT_HEAD_DIM) ** -0.5),
        'a_sinks': nrm((nA, ATT_HEADS), 1.0),
        'b_w_in': nrm((nB, Dm, gdn_in), Dm ** -0.5),
        'b_conv': nrm((nB, GDN_CONV, gdn_conv_ch), GDN_CONV ** -0.5),
        'b_a_log': jnp.log(uni((nB, GDN_V_HEADS), 1.0, 16.0)),
        'b_dt_bias': dt + jnp.log(-jnp.expm1(-dt)),
        'b_norm': 1.0 + nrm((nB, GDN_HEAD_DIM), 0.02),
        'b_w_o': nrm((nB, GDN_VAL_DIM, Dm), GDN_VAL_DIM ** -0.5),
        'c_w_pw1': nrm((nC, Dm, 2 * Dm), Dm ** -0.5),
        'c_b_pw1': nrm((nC, 2 * Dm), 0.02),
        'c_w_dw': nrm((nC, CONF_KERNEL, Dm), CONF_KERNEL ** -0.5),
        'c_b_dw': nrm((nC, Dm), 0.02),
        'c_ln_g': 1.0 + nrm((nC, Dm), 0.02),
        'c_ln_b': nrm((nC, Dm), 0.02),
        'c_w_pw2': nrm((nC, Dm, Dm), Dm ** -0.5),
        'c_b_pw2': nrm((nC, Dm), 0.02),
        'd_w_in': nrm((nD, Dm, 3 * Dm), Dm ** -0.5),
        'd_w_conv': nrm((nD, SHORT_KERNEL, Dm), SHORT_KERNEL ** -0.5),
        'd_w_out': nrm((nD, Dm, Dm), Dm ** -0.5),
        'f_w_gate': nrm((DEPTH, Dm, D_FF), Dm ** -0.5),
        'f_w_up': nrm((DEPTH, Dm, D_FF), Dm ** -0.5),
        'f_w_conv': nrm((DEPTH, FFN_KERNEL, D_FF), FFN_KERNEL ** -0.5),
        'f_b_conv': nrm((DEPTH, D_FF), 0.02),
        'f_w_down': nrm((DEPTH, D_FF, Dm), D_FF ** -0.5),
    }


def reference(x, mix_norm, ffn_norm, final_norm,
              a_w_qkv, a_w_o, a_sinks,
              b_w_in, b_conv, b_a_log, b_dt_bias, b_norm, b_w_o,
              c_w_pw1, c_b_pw1, c_w_dw, c_b_dw, c_ln_g, c_ln_b, c_w_pw2, c_b_pw2,
              d_w_in, d_w_conv, d_w_out,
              f_w_gate, f_w_up, f_w_conv, f_b_conv, f_w_down):
    for i in range(DEPTH):
        kind, j = i % N_MIXERS, i // N_MIXERS
        h = rms_norm(x, mix_norm[i])
        if kind == 0:
            y = swa_sink_attention(h, a_w_qkv[j], a_w_o[j], a_sinks[j])
        elif kind == 1:
            y = gated_deltanet(h, b_w_in[j], b_conv[j], b_a_log[j], b_dt_bias[j], b_norm[j], b_w_o[j])
        elif kind == 2:
            y = conformer_conv(h, c_w_pw1[j], c_b_pw1[j], c_w_dw[j], c_b_dw[j],
                               c_ln_g[j], c_ln_b[j], c_w_pw2[j], c_b_pw2[j])
        else:
            y = short_gated_conv(h, d_w_in[j], d_w_conv[j], d_w_out[j])
        x = x + y
        h = rms_norm(x, ffn_norm[i])
        x = x + conv_glu_ffn(h, f_w_gate[i], f_w_up[i], f_w_conv[i], f_b_conv[i], f_w_down[i])
    return rms_norm(x, final_norm)
```

```python
import functools
import math

import jax
import jax.numpy as jnp
from jax import lax
from jax.experimental import pallas as pl
from jax.experimental.pallas import tpu as pltpu

F32 = jnp.float32
BF16 = jnp.bfloat16

NORM_EPS = 1e-6
LN_EPS = 1e-5
L2_EPS = 1e-6
ROPE_THETA = 10000.0
ATT_HEAD_DIM = 128
ATT_GROUP = 4
WINDOW = 128
GDN_HEAD_DIM = 128
GDN_CHUNK = 64
GDN_SOLVE_BLOCK = 16
CONF_KERNEL = 31

LANES = 128
SUBLANES = 8
VMEM_CAP_BYTES = 58 * 1024 * 1024
VMEM_SLACK_BYTES = 6 * 1024 * 1024


def _cparams(n_axes, vmem_bytes):
    limit = int(min(VMEM_CAP_BYTES, max(vmem_bytes + VMEM_SLACK_BYTES, 16 * 1024 * 1024)))
    return pltpu.CompilerParams(dimension_semantics=("arbitrary",) * n_axes, vmem_limit_bytes=limit)


def _nbytes(shape, dtype):
    return math.prod(shape) * jnp.dtype(dtype).itemsize


def _silu(x):
    return x * jax.nn.sigmoid(x)


def _mm(a, b):
    return jnp.dot(a.astype(BF16), b.astype(BF16), preferred_element_type=F32)


def _mm_nt(a, b):
    return lax.dot_general(a.astype(BF16), b.astype(BF16), (((1,), (1,)), ((), ())),
                           preferred_element_type=F32)


def _rmsnorm_kernel(x_ref, g_ref, o_ref):
    x = x_ref[...]
    ms = jnp.mean(x * x, axis=-1, keepdims=True)
    o_ref[...] = (x * lax.rsqrt(ms + NORM_EPS) * g_ref[...]).astype(o_ref.dtype)


def rmsnorm(x, g, out_dtype, tm=512):
    S, D = x.shape
    tm = min(tm, S)
    vmem = 2 * (_nbytes((tm, D), F32) + _nbytes((tm, D), out_dtype)) + _nbytes((tm, D), F32)
    return pl.pallas_call(
        _rmsnorm_kernel,
        grid=(S // tm,),
        in_specs=[pl.BlockSpec((tm, D), lambda m: (m, 0)),
                  pl.BlockSpec((1, D), lambda m: (0, 0))],
        out_specs=pl.BlockSpec((tm, D), lambda m: (m, 0)),
        out_shape=jax.ShapeDtypeStruct((S, D), out_dtype),
        compiler_params=_cparams(1, vmem),
        name="rmsnorm",
    )(x, g.reshape(1, D))


def _causal_taps(ext_ref, w_ref, n_taps, tm, row_chunk, emit):
    for r in range(0, tm, row_chunk):
        y = None
        for k in range(n_taps):
            off = SUBLANES - (n_taps - 1 - k) + r
            term = ext_ref[off:off + row_chunk, :] * w_ref[k:k + 1, :]
            y = term if y is None else y + term
        emit(r, y)


def _proj_kernel(*refs, n_w, n_p, mode, tm, tn, n_taps, row_chunk, q_tiles):
    h_ref = refs[0]
    w_refs = refs[1:1 + n_w]
    p_refs = refs[1 + n_w:1 + n_w + n_p]
    o_ref = refs[1 + n_w + n_p]
    wbf_ref = refs[2 + n_w + n_p]
    ext_ref = refs[3 + n_w + n_p] if n_taps else None
    n = pl.program_id(0)
    m = pl.program_id(1)

    @pl.when(m == 0)
    def _():
        for i in range(n_w):
            wbf_ref[i] = w_refs[i][...].astype(BF16)
        if n_taps:
            ext_ref[0:SUBLANES, :] = jnp.zeros((SUBLANES, tn), F32)

    h = h_ref[...]

    def dot(i):
        return jnp.dot(h, wbf_ref[i], preferred_element_type=F32)

    def carry():
        ext_ref[0:SUBLANES, :] = ext_ref[tm:tm + SUBLANES, :]

    if mode == "plain":
        o_ref[...] = dot(0).astype(o_ref.dtype)
    elif mode == "residual":
        acc = dot(0)
        if n_p == 2:
            acc = acc + p_refs[1][...]
        o_ref[...] = p_refs[0][...] + acc
    elif mode == "glu_bias":
        val = dot(0) + p_refs[0][...]
        gate = dot(1) + p_refs[1][...]
        o_ref[...] = (val * jax.nn.sigmoid(gate)).astype(o_ref.dtype)
    elif mode == "conv_silu":
        ext_ref[SUBLANES:SUBLANES + tm, :] = dot(0)

        def emit(r, y):
            o_ref[r:r + row_chunk, :] = _silu(y).astype(o_ref.dtype)
        _causal_taps(ext_ref, p_refs[0], n_taps, tm, row_chunk, emit)
        carry()
    elif mode == "conv_silu_l2":
        ext_ref[SUBLANES:SUBLANES + tm, :] = dot(0)
        scale = jnp.where(n < q_tiles, GDN_HEAD_DIM ** -0.5, 1.0).astype(F32)

        def emit(r, y):
            y = _silu(y)
            for j in range(0, tn, GDN_HEAD_DIM):
                yh = y[:, j:j + GDN_HEAD_DIM]
                inv = lax.rsqrt(jnp.sum(yh * yh, axis=-1, keepdims=True) + L2_EPS)
                o_ref[r:r + row_chunk, j:j + GDN_HEAD_DIM] = (yh * inv * scale).astype(o_ref.dtype)
        _causal_taps(ext_ref, p_refs[0], n_taps, tm, row_chunk, emit)
        carry()
    elif mode == "short_conv":
        ext_ref[SUBLANES:SUBLANES + tm, :] = dot(1) * dot(2)
        bg = dot(0)

        def emit(r, y):
            o_ref[r:r + row_chunk, :] = (bg[r:r + row_chunk, :] * y).astype(o_ref.dtype)
        _causal_taps(ext_ref, p_refs[0], n_taps, tm, row_chunk, emit)
        carry()
    elif mode == "conv_glu":
        ext_ref[SUBLANES:SUBLANES + tm, :] = dot(0)
        up = dot(1)
        bias = p_refs[1][...]

        def emit(r, y):
            o_ref[r:r + row_chunk, :] = (_silu(y + bias) * up[r:r + row_chunk, :]).astype(o_ref.dtype)
        _causal_taps(ext_ref, p_refs[0], n_taps, tm, row_chunk, emit)
        carry()
    else:
        raise ValueError(mode)


def proj(h, weights, params, mode, n_out, out_dtype, *, tn, tm=1024, n_taps=0, q_tiles=0,
         row_chunk=128):
    S, K = h.shape
    tm = min(tm, S)
    assert S % tm == 0 and n_out % tn == 0 and tm % row_chunk == 0
    n_w = len(weights)
    in_specs = [pl.BlockSpec((tm, K), lambda n, m: (m, 0))]
    args = [h]
    vmem = 2 * _nbytes((tm, K), h.dtype)
    for w, layer, off in weights:
        assert w.shape[1] == K and off % tn == 0
        in_specs.append(pl.BlockSpec((None, K, tn), lambda n, m, l=layer, o=off // tn: (l, 0, n + o)))
        args.append(w)
        vmem += 2 * _nbytes((K, tn), w.dtype) + _nbytes((K, tn), BF16)
    for p, off, kind in params:
        if kind == "col":
            in_specs.append(pl.BlockSpec((p.shape[0], tn), lambda n, m, o=off // tn: (0, n + o)))
            vmem += 2 * _nbytes((SUBLANES, tn), F32)
        else:
            in_specs.append(pl.BlockSpec((tm, tn), lambda n, m, o=off // tn: (m, n + o)))
            vmem += 2 * _nbytes((tm, tn), p.dtype)
        args.append(p)
    scratch = [pltpu.VMEM((n_w, K, tn), BF16)]
    if n_taps:
        scratch.append(pltpu.VMEM((tm + SUBLANES, tn), F32))
        vmem += _nbytes((tm + SUBLANES, tn), F32)
    vmem += 2 * _nbytes((tm, tn), out_dtype) + n_w * _nbytes((tm, tn), F32)
    kern = functools.partial(_proj_kernel, n_w=n_w, n_p=len(params), mode=mode, tm=tm, tn=tn,
                             n_taps=n_taps, row_chunk=row_chunk, q_tiles=q_tiles)
    return pl.pallas_call(
        kern,
        grid=(n_out // tn, S // tm),
        in_specs=in_specs,
        out_specs=pl.BlockSpec((tm, tn), lambda n, m: (m, n)),
        out_shape=jax.ShapeDtypeStruct((S, n_out), out_dtype),
        scratch_shapes=scratch,
        compiler_params=_cparams(2, vmem),
        name="proj_" + mode,
    )(*args)


def _down_kernel(a_ref, w_ref, x_ref, o_ref):
    o_ref[...] = x_ref[...] + jnp.dot(a_ref[...], w_ref[...], preferred_element_type=F32)


def down_residual(a, w_bf16, x, *, tm=512, tn=256):
    S, K = a.shape
    N = w_bf16.shape[1]
    tm = min(tm, S)
    vmem = 2 * (_nbytes((tm, K), a.dtype) + _nbytes((K, tn), BF16) + 2 * _nbytes((tm, tn), F32))
    vmem += _nbytes((tm, tn), F32)
    return pl.pallas_call(
        _down_kernel,
        grid=(S // tm, N // tn),
        in_specs=[pl.BlockSpec((tm, K), lambda m, n: (m, 0)),
                  pl.BlockSpec((K, tn), lambda m, n: (0, n)),
                  pl.BlockSpec((tm, tn), lambda m, n: (m, n))],
        out_specs=pl.BlockSpec((tm, tn), lambda m, n: (m, n)),
        out_shape=jax.ShapeDtypeStruct((S, N), F32),
        compiler_params=_cparams(2, vmem),
        name="down_residual",
    )(a, w_bf16, x)


def _swa_kernel(sink_ref, q_ref, k_ref, v_ref, cos_ref, sin_ref, o_ref, kprev_ref, vprev_ref,
                *, n_kv):
    b = pl.program_id(0)
    D = ATT_HEAD_DIM
    blk = q_ref.shape[0]

    @pl.when(b == 0)
    def _():
        kprev_ref[...] = jnp.zeros(kprev_ref.shape, F32)
        vprev_ref[...] = jnp.zeros(vprev_ref.shape, F32)

    cos = cos_ref[...]
    sin = sin_ref[...]

    def rope(x):
        return x * cos + pltpu.roll(x, D // 2, 1) * sin

    qi = lax.broadcasted_iota(jnp.int32, (blk, 2 * blk), 0)
    kj = lax.broadcasted_iota(jnp.int32, (blk, 2 * blk), 1)
    rel = blk + qi - kj
    valid = (rel >= 0) & (rel < WINDOW) & ((kj >= blk) | (b > 0))
    scale = D ** -0.5

    for g in range(n_kv):
        kc = rope(k_ref[:, g * D:(g + 1) * D])
        vc = v_ref[:, g * D:(g + 1) * D]
        kband = jnp.concatenate([kprev_ref[:, g * D:(g + 1) * D], kc], axis=0).astype(BF16)
        vband = jnp.concatenate([vprev_ref[:, g * D:(g + 1) * D], vc], axis=0).astype(BF16)
        for gg in range(ATT_GROUP):
            hh = g * ATT_GROUP + gg
            qh = rope(q_ref[:, hh * D:(hh + 1) * D])
            s = _mm_nt(qh, kband) * scale
            s = jnp.where(valid, s, -jnp.inf)
            sink = sink_ref[hh]
            mx = jnp.maximum(jnp.max(s, axis=-1, keepdims=True), sink)
            p = jnp.exp(s - mx)
            denom = jnp.sum(p, axis=-1, keepdims=True) + jnp.exp(sink - mx)
            o = _mm(p / denom, vband)
            o_ref[:, hh * D:(hh + 1) * D] = o.astype(o_ref.dtype)
        kprev_ref[:, g * D:(g + 1) * D] = kc
        vprev_ref[:, g * D:(g + 1) * D] = vc


def swa_attention(qkv, sinks, cos_full, sin_signed, n_heads, n_kv):
    S = qkv.shape[0]
    D = ATT_HEAD_DIM
    blk = WINDOW
    qw, kw = n_heads * D, n_kv * D
    vmem = 2 * (_nbytes((blk, qw), F32) + 2 * _nbytes((blk, kw), F32) + 2 * _nbytes((blk, D), F32)
                + _nbytes((blk, qw), BF16)) + 2 * _nbytes((blk, kw), F32)
    return pl.pallas_call(
        functools.partial(_swa_kernel, n_kv=n_kv),
        grid=(S // blk,),
        in_specs=[pl.BlockSpec(memory_space=pltpu.SMEM),
                  pl.BlockSpec((blk, qw), lambda b: (b, 0)),
                  pl.BlockSpec((blk, kw), lambda b: (b, qw // kw)),
                  pl.BlockSpec((blk, kw), lambda b: (b, qw // kw + 1)),
                  pl.BlockSpec((blk, D), lambda b: (b, 0)),
                  pl.BlockSpec((blk, D), lambda b: (b, 0))],
        out_specs=pl.BlockSpec((blk, qw), lambda b: (b, 0)),
        out_shape=jax.ShapeDtypeStruct((S, qw), BF16),
        scratch_shapes=[pltpu.VMEM((blk, kw), F32), pltpu.VMEM((blk, kw), F32)],
        compiler_params=_cparams(1, vmem),
        name="swa_attention",
    )(sinks, qkv, qkv, qkv, cos_full, sin_signed)


def _delta_kernel(q_ref, k_ref, v_ref, z_ref, ba_ref, prm_ref, nw_ref, o_ref, st_ref,
                  *, n_v, rep):
    c = pl.program_id(0)
    C = q_ref.shape[0]
    D = GDN_HEAD_DIM

    @pl.when(c == 0)
    def _():
        st_ref[...] = jnp.zeros(st_ref.shape, F32)

    ba = ba_ref[...]
    beta_t = jax.nn.sigmoid(ba)
    a_sh = ba + prm_ref[1:2, :]
    softplus = jnp.maximum(a_sh, 0.0) + jnp.log(1.0 + jnp.exp(-jnp.abs(a_sh)))
    g_t = -jnp.exp(prm_ref[0:1, :]) * softplus
    row = lax.broadcasted_iota(jnp.int32, (C, LANES), 0)
    gc = g_t
    s = 1
    while s < C:
        gc = gc + jnp.where(row >= s, pltpu.roll(gc, s, 0), 0.0)
        s *= 2
    glast = gc[C - 1:C, :]
    e_gc = jnp.exp(gc)
    e_rest = jnp.exp(glast - gc)
    e_last = jnp.exp(glast)
    gc_t = gc.T

    ii = lax.broadcasted_iota(jnp.int32, (C, C), 0)
    jj = lax.broadcasted_iota(jnp.int32, (C, C), 1)
    lower = ii >= jj
    strict = ii > jj
    same_blk = (ii // GDN_SOLVE_BLOCK) == (jj // GDN_SOLVE_BLOCK)
    nw = nw_ref[...]

    for i in range(n_v):
        j = i // rep
        qh = q_ref[:, j * D:(j + 1) * D]
        kh = k_ref[:, j * D:(j + 1) * D]
        vh = v_ref[:, i * D:(i + 1) * D]
        bcol = beta_t[:, i:i + 1]
        gi = n_v + i
        dec = jnp.exp(jnp.where(lower, gc[:, gi:gi + 1] - gc_t[gi:gi + 1, :], -jnp.inf))
        kb = kh * bcol
        a_kk = jnp.where(strict, _mm_nt(kb, kh) * dec, 0.0)
        attn = _mm_nt(qh, kh) * dec

        a_d = jnp.where(same_blk, a_kk, 0.0)
        a_o = a_kk - a_d
        tp = -a_d
        pw = tp
        for _ in range(int(math.log2(GDN_SOLVE_BLOCK)) - 1):
            pw = _mm(pw, pw)
            tp = tp + pw + _mm(tp, pw)
        nn = a_o + _mm(tp, a_o)
        rhs = jnp.concatenate([vh * bcol, kb * e_gc[:, gi:gi + 1]], axis=1)
        y = rhs + _mm(tp, rhs)
        n2 = _mm(nn, nn)
        zz = y + _mm(n2, y)
        x = zz - _mm(nn, zz)
        u = x[:, :D]
        w = x[:, D:]

        st = st_ref[i]
        v_new = u - _mm(w, st)
        out = _mm(qh * e_gc[:, gi:gi + 1], st) + _mm(attn, v_new)
        kd = kh * e_rest[:, gi:gi + 1]
        st_ref[i] = st * e_last[:, gi:gi + 1] + _mm(kd.T, v_new)

        ms = jnp.mean(out * out, axis=-1, keepdims=True)
        zh = z_ref[:, i * D:(i + 1) * D]
        o_ref[:, i * D:(i + 1) * D] = (out * lax.rsqrt(ms + NORM_EPS) * nw * _silu(zh)).astype(o_ref.dtype)


def gated_delta(qk, v, z, ba, prm, norm_w, n_qk, n_v):
    S = qk.shape[0]
    D = GDN_HEAD_DIM
    C = GDN_CHUNK
    qw, vw = n_qk * D, n_v * D
    assert GDN_SOLVE_BLOCK * 4 == C
    vmem = 2 * (2 * _nbytes((C, qw), F32) + 2 * _nbytes((C, vw), F32) + _nbytes((C, vw), BF16))
    vmem += _nbytes((n_v, D, D), F32)
    return pl.pallas_call(
        functools.partial(_delta_kernel, n_v=n_v, rep=n_v // n_qk),
        grid=(S // C,),
        in_specs=[pl.BlockSpec((C, qw), lambda c: (c, 0)),
                  pl.BlockSpec((C, qw), lambda c: (c, 1)),
                  pl.BlockSpec((C, vw), lambda c: (c, 0)),
                  pl.BlockSpec((C, vw), lambda c: (c, 0)),
                  pl.BlockSpec((C, LANES), lambda c: (c, 0)),
                  pl.BlockSpec((2, LANES), lambda c: (0, 0)),
                  pl.BlockSpec((1, D), lambda c: (0, 0))],
        out_specs=pl.BlockSpec((C, vw), lambda c: (c, 0)),
        out_shape=jax.ShapeDtypeStruct((S, vw), BF16),
        scratch_shapes=[pltpu.VMEM((n_v, D, D), F32)],
        compiler_params=_cparams(1, vmem),
        name="gated_delta",
    )(qk, qk, v, z, ba, prm, norm_w.reshape(1, D))


CONF_HIST = 32


def _conf_kernel(u_ref, w_ref, bdw_ref, g_ref, b_ref, o_ref, ext_ref, y_ref, *, tm, cw, n_c,
                 row_chunk):
    m = pl.program_id(0)
    c = pl.program_id(1)

    @pl.when(m == 0)
    def _():
        ext_ref[c, 0:CONF_HIST, :] = jnp.zeros((CONF_HIST, cw), F32)

    ext_ref[c, CONF_HIST:CONF_HIST + tm, :] = u_ref[...]
    bias = bdw_ref[...]
    for r in range(0, tm, row_chunk):
        y = None
        for k in range(CONF_KERNEL):
            off = CONF_HIST - (CONF_KERNEL - 1 - k) + r
            term = ext_ref[c, off:off + row_chunk, :] * w_ref[k:k + 1, :]
            y = term if y is None else y + term
        y_ref[c, r:r + row_chunk, :] = y + bias
    ext_ref[c, 0:CONF_HIST, :] = ext_ref[c, tm:tm + CONF_HIST, :]

    @pl.when(c == n_c - 1)
    def _():
        d = n_c * cw
        tot = jnp.zeros((tm, 1), F32)
        for j in range(n_c):
            tot = tot + jnp.sum(y_ref[j], axis=-1, keepdims=True)
        mean = tot / d
        var = jnp.zeros((tm, 1), F32)
        for j in range(n_c):
            yc = y_ref[j] - mean
            var = var + jnp.sum(yc * yc, axis=-1, keepdims=True)
        inv = lax.rsqrt(var / d + LN_EPS)
        for j in range(n_c):
            zz = (y_ref[j] - mean) * inv * g_ref[:, j * cw:(j + 1) * cw] + b_ref[:, j * cw:(j + 1) * cw]
            o_ref[:, j * cw:(j + 1) * cw] = _silu(zz).astype(o_ref.dtype)


def conformer_mid(u, w_dw, b_dw, ln_g, ln_b, *, tm=256, cw=512, row_chunk=32):
    S, D = u.shape
    tm, cw = min(tm, S), min(cw, D)
    n_c = D // cw
    vmem = 2 * (_nbytes((tm, cw), F32) + _nbytes((CONF_HIST, cw), F32) + _nbytes((tm, D), BF16))
    vmem += _nbytes((n_c, tm + CONF_HIST, cw), F32) + 3 * _nbytes((n_c, tm, cw), F32)
    kern = functools.partial(_conf_kernel, tm=tm, cw=cw, n_c=n_c, row_chunk=row_chunk)
    return pl.pallas_call(
        kern,
        grid=(S // tm, n_c),
        in_specs=[pl.BlockSpec((tm, cw), lambda m, c: (m, c)),
                  pl.BlockSpec((CONF_KERNEL, cw), lambda m, c: (0, c)),
                  pl.BlockSpec((1, cw), lambda m, c: (0, c)),
                  pl.BlockSpec((1, D), lambda m, c: (0, 0)),
                  pl.BlockSpec((1, D), lambda m, c: (0, 0))],
        out_specs=pl.BlockSpec((tm, D), lambda m, c: (m, 0)),
        out_shape=jax.ShapeDtypeStruct((S, D), BF16),
        scratch_shapes=[pltpu.VMEM((n_c, tm + CONF_HIST, cw), F32),
                        pltpu.VMEM((n_c, tm, cw), F32)],
        compiler_params=_cparams(2, vmem),
        name="conformer_mid",
    )(u, w_dw, b_dw.reshape(1, D), ln_g.reshape(1, D), ln_b.reshape(1, D))


def _rope_tables(S):
    half = ATT_HEAD_DIM // 2
    inv = jnp.power(ROPE_THETA, -jnp.arange(half, dtype=F32) / half)
    ang = jnp.arange(S).astype(F32)[:, None] * inv[None, :]
    cos, sin = jnp.cos(ang), jnp.sin(ang)
    return jnp.concatenate([cos, cos], axis=1), jnp.concatenate([-sin, sin], axis=1)


def _wide_tile(n):
    return 512 if n % 512 == 0 else 256


def _mixer_a(h, x, w_qkv, w_o, sinks, j):
    S, Dm = x.shape
    n_heads = w_o.shape[1] // ATT_HEAD_DIM
    n_kv = n_heads // ATT_GROUP
    n_qkv = w_qkv.shape[2]
    qkv = proj(h, [(w_qkv, j, 0)], [], "plain", n_qkv, F32, tn=_wide_tile(n_qkv))
    cos_full, sin_signed = _rope_tables(S)
    o = swa_attention(qkv, sinks[j], cos_full, sin_signed, n_heads, n_kv)
    return proj(o, [(w_o, j, 0)], [(x, 0, "tile")], "residual", Dm, F32, tn=_wide_tile(Dm))


def _mixer_b(h, x, w_in, conv_w, a_log, dt_bias, norm_w, w_o, j):
    S, Dm = x.shape
    D = GDN_HEAD_DIM
    conv_w, a_log, dt_bias, norm_w = conv_w[j], a_log[j], dt_bias[j], norm_w[j]
    n_v = a_log.shape[0]
    val_dim = n_v * D
    key_dim = (conv_w.shape[1] - val_dim) // 2
    n_qk = key_dim // D
    n_qkv = 2 * key_dim + val_dim
    tn = _wide_tile(key_dim)
    n_taps = conv_w.shape[0]
    qk = proj(h, [(w_in, j, 0)], [(conv_w, 0, "col")], "conv_silu_l2", 2 * key_dim, F32, tn=tn,
              n_taps=n_taps, q_tiles=key_dim // tn)
    v = proj(h, [(w_in, j, 2 * key_dim)], [(conv_w, 2 * key_dim, "col")], "conv_silu", val_dim, F32,
             tn=tn, n_taps=n_taps)
    z = proj(h, [(w_in, j, n_qkv)], [], "plain", val_dim, F32, tn=tn)
    w_ba = jnp.pad(w_in[j][:, n_qkv + val_dim:], ((0, 0), (0, LANES - 2 * n_v)))
    ba = proj(h, [(w_ba[None], 0, 0)], [], "plain", LANES, F32, tn=LANES)
    prm = jnp.zeros((2, LANES), F32)
    prm = prm.at[0, n_v:2 * n_v].set(a_log).at[1, n_v:2 * n_v].set(dt_bias)
    o = gated_delta(qk, v, z, ba, prm, norm_w, n_qk, n_v)
    return proj(o, [(w_o, j, 0)], [(x, 0, "tile")], "residual", Dm, F32, tn=_wide_tile(Dm))


def _mixer_c(h, x, w_pw1, b_pw1, w_dw, b_dw, ln_g, ln_b, w_pw2, b_pw2, j):
    S, Dm = x.shape
    b1 = b_pw1[j].reshape(1, -1)
    u = proj(h, [(w_pw1, j, 0), (w_pw1, j, Dm)], [(b1, 0, "col"), (b1, Dm, "col")], "glu_bias", Dm,
             F32, tn=256)
    mid = conformer_mid(u, w_dw[j], b_dw[j], ln_g[j], ln_b[j])
    return proj(mid, [(w_pw2, j, 0)], [(x, 0, "tile"), (b_pw2[j].reshape(1, -1), 0, "col")],
                "residual", Dm, F32, tn=_wide_tile(Dm))


def _mixer_d(h, x, w_in, w_conv, w_out, j):
    S, Dm = x.shape
    mid = proj(h, [(w_in, j, 0), (w_in, j, Dm), (w_in, j, 2 * Dm)], [(w_conv[j], 0, "col")],
               "short_conv", Dm, BF16, tn=256, n_taps=w_conv.shape[1])
    return proj(mid, [(w_out, j, 0)], [(x, 0, "tile")], "residual", Dm, F32, tn=_wide_tile(Dm))


def _ffn(h, x, w_gate, w_up, w_conv, b_conv, w_down, i):
    d_ff = w_gate.shape[2]
    mid = proj(h, [(w_gate, i, 0), (w_up, i, 0)],
               [(w_conv[i], 0, "col"), (b_conv[i].reshape(1, -1), 0, "col")],
               "conv_glu", d_ff, BF16, tn=256, n_taps=w_conv.shape[1])
    return down_residual(mid, w_down[i].astype(BF16), x)


def kernel(x, mix_norm, ffn_norm, final_norm, a_w_qkv, a_w_o, a_sinks, b_w_in, b_conv, b_a_log, b_dt_bias, b_norm, b_w_o, c_w_pw1, c_b_pw1, c_w_dw, c_b_dw, c_ln_g, c_ln_b, c_w_pw2, c_b_pw2, d_w_in, d_w_conv, d_w_out, f_w_gate, f_w_up, f_w_conv, f_b_conv, f_w_down):
    B, S, Dm = x.shape
    depth = mix_norm.shape[0]
    outs = []
    for bi in range(B):
        xb = x[bi]
        for i in range(depth):
            kind, j = i % 4, i // 4
            h = rmsnorm(xb, mix_norm[i], BF16)
            if kind == 0:
                xb = _mixer_a(h, xb, a_w_qkv, a_w_o, a_sinks, j)
            elif kind == 1:
                xb = _mixer_b(h, xb, b_w_in, b_conv, b_a_log, b_dt_bias, b_norm, b_w_o, j)
            elif kind == 2:
                xb = _mixer_c(h, xb, c_w_pw1, c_b_pw1, c_w_dw, c_b_dw, c_ln_g, c_ln_b, c_w_pw2,
                              c_b_pw2, j)
            else:
                xb = _mixer_d(h, xb, d_w_in, d_w_conv, d_w_out, j)
            h = rmsnorm(xb, ffn_norm[i], BF16)
            xb = _ffn(h, xb, f_w_gate, f_w_up, f_w_conv, f_b_conv, f_w_down, i)
        outs.append(rmsnorm(xb, final_norm, F32))
    return jnp.stack(outs, axis=0)
```

```python
import functools
import math

import jax
import jax.numpy as jnp
from jax import lax
from jax.experimental import pallas as pl
from jax.experimental.pallas import tpu as pltpu

F32 = jnp.float32
BF16 = jnp.bfloat16

NORM_EPS = 1e-6
LN_EPS = 1e-5
L2_EPS = 1e-6
ROPE_THETA = 10000.0
ATT_HEAD_DIM = 128
ATT_GROUP = 4
WINDOW = 128
GDN_HEAD_DIM = 128
GDN_CHUNK = 64
GDN_SOLVE_BLOCK = 16
GDN_HEAD_GROUP = 16
CONF_KERNEL = 31

LANES = 128
SUBLANES = 8
VMEM_CAP_BYTES = 58 * 1024 * 1024
VMEM_SLACK_BYTES = 6 * 1024 * 1024


def _cparams(n_axes, vmem_bytes):
    limit = int(min(VMEM_CAP_BYTES, max(vmem_bytes + VMEM_SLACK_BYTES, 16 * 1024 * 1024)))
    return pltpu.CompilerParams(dimension_semantics=("arbitrary",) * n_axes, vmem_limit_bytes=limit)


def _nbytes(shape, dtype):
    return math.prod(shape) * jnp.dtype(dtype).itemsize


def _silu(x):
    return x * jax.nn.sigmoid(x)


def _mm(a, b):
    return jnp.dot(a.astype(BF16), b.astype(BF16), preferred_element_type=F32)


def _mm_nt(a, b):
    return lax.dot_general(a.astype(BF16), b.astype(BF16), (((1,), (1,)), ((), ())),
                           preferred_element_type=F32)


def _rmsnorm_kernel(x_ref, g_ref, o_ref):
    x = x_ref[...]
    ms = jnp.mean(x * x, axis=-1, keepdims=True)
    o_ref[...] = (x * lax.rsqrt(ms + NORM_EPS) * g_ref[...]).astype(o_ref.dtype)


def rmsnorm(x, g, out_dtype, tm=512):
    S, D = x.shape
    tm = min(tm, S)
    vmem = 2 * (_nbytes((tm, D), F32) + _nbytes((tm, D), out_dtype)) + _nbytes((tm, D), F32)
    return pl.pallas_call(
        _rmsnorm_kernel,
        grid=(S // tm,),
        in_specs=[pl.BlockSpec((tm, D), lambda m: (m, 0)),
                  pl.BlockSpec((1, D), lambda m: (0, 0))],
        out_specs=pl.BlockSpec((tm, D), lambda m: (m, 0)),
        out_shape=jax.ShapeDtypeStruct((S, D), out_dtype),
        compiler_params=_cparams(1, vmem),
        name="rmsnorm",
    )(x, g.reshape(1, D))


def _causal_taps(ext_ref, w_ref, n_taps, r, rows):
    y = None
    for k in range(n_taps):
        off = SUBLANES - (n_taps - 1 - k) + r
        term = ext_ref[off:off + rows, :] * w_ref[k:k + 1, :]
        y = term if y is None else y + term
    return y


def _proj_kernel(*refs, n_w, n_p, mode, tm, tn, n_taps, row_block, row_chunk, q_tiles):
    h_ref = refs[0]
    w_refs = refs[1:1 + n_w]
    p_refs = refs[1 + n_w:1 + n_w + n_p]
    o_ref = refs[1 + n_w + n_p]
    wbf_ref = refs[2 + n_w + n_p]
    ext_ref = refs[3 + n_w + n_p] if n_taps else None
    n = pl.program_id(0)
    m = pl.program_id(1)

    @pl.when(m == 0)
    def _():
        for i in range(n_w):
            wbf_ref[i] = w_refs[i][...].astype(BF16)
        if n_taps:
            ext_ref[0:SUBLANES, :] = jnp.zeros((SUBLANES, tn), F32)

    def dot(i, r0):
        return jnp.dot(h_ref[r0:r0 + row_block, :], wbf_ref[i], preferred_element_type=F32)

    def to_ext(r0, val):
        ext_ref[SUBLANES + r0:SUBLANES + r0 + row_block, :] = val

    def chunks(r0):
        return [(r, r - r0) for r in range(r0, r0 + row_block, row_chunk)]

    if mode == "plain":
        def matmul(r0):
            return dot(0, r0)

        def finish(r0, acc):
            o_ref[r0:r0 + row_block, :] = acc.astype(o_ref.dtype)
    elif mode == "residual":
        def matmul(r0):
            return dot(0, r0)

        def finish(r0, acc):
            if n_p == 2:
                acc = acc + p_refs[1][...]
            o_ref[r0:r0 + row_block, :] = p_refs[0][r0:r0 + row_block, :] + acc
    elif mode == "glu_bias":
        def matmul(r0):
            return dot(0, r0), dot(1, r0)

        def finish(r0, accs):
            val = accs[0] + p_refs[0][...]
            gate = accs[1] + p_refs[1][...]
            o_ref[r0:r0 + row_block, :] = (val * jax.nn.sigmoid(gate)).astype(o_ref.dtype)
    elif mode == "conv_silu":
        def matmul(r0):
            to_ext(r0, dot(0, r0))

        def finish(r0, _):
            for r, _ in chunks(r0):
                y = _causal_taps(ext_ref, p_refs[0], n_taps, r, row_chunk)
                o_ref[r:r + row_chunk, :] = _silu(y).astype(o_ref.dtype)
    elif mode == "conv_silu_l2":
        scale = jnp.where(n < q_tiles, GDN_HEAD_DIM ** -0.5, 1.0).astype(F32)

        def matmul(r0):
            to_ext(r0, dot(0, r0))

        def finish(r0, _):
            for r, _ in chunks(r0):
                y = _silu(_causal_taps(ext_ref, p_refs[0], n_taps, r, row_chunk))
                for j in range(0, tn, GDN_HEAD_DIM):
                    yh = y[:, j:j + GDN_HEAD_DIM]
                    inv = lax.rsqrt(jnp.sum(yh * yh, axis=-1, keepdims=True) + L2_EPS)
                    o_ref[r:r + row_chunk, j:j + GDN_HEAD_DIM] = (yh * inv * scale).astype(o_ref.dtype)
    elif mode == "short_conv":
        def matmul(r0):
            to_ext(r0, dot(1, r0) * dot(2, r0))
            return dot(0, r0)

        def finish(r0, bg):
            for r, rr in chunks(r0):
                y = _causal_taps(ext_ref, p_refs[0], n_taps, r, row_chunk)
                o_ref[r:r + row_chunk, :] = (bg[rr:rr + row_chunk, :] * y).astype(o_ref.dtype)
    elif mode == "conv_glu":
        def matmul(r0):
            to_ext(r0, dot(0, r0))
            return dot(1, r0)

        def finish(r0, up):
            bias = p_refs[1][...]
            for r, rr in chunks(r0):
                y = _causal_taps(ext_ref, p_refs[0], n_taps, r, row_chunk)
                o_ref[r:r + row_chunk, :] = (_silu(y + bias) * up[rr:rr + row_chunk, :]).astype(o_ref.dtype)
    else:
        raise ValueError(mode)

    pending = None
    for r0 in range(0, tm, row_block):
        acc = matmul(r0)
        if pending is not None:
            finish(*pending)
        pending = (r0, acc)
    finish(*pending)
    if n_taps:
        ext_ref[0:SUBLANES, :] = ext_ref[tm:tm + SUBLANES, :]


def proj(h, weights, params, mode, n_out, out_dtype, *, tn, tm=1024, n_taps=0, q_tiles=0,
         row_block=256, row_chunk=128):
    S, K = h.shape
    tm = min(tm, S)
    row_block = min(row_block, tm)
    assert S % tm == 0 and n_out % tn == 0 and tm % row_block == 0 and row_block % row_chunk == 0
    n_w = len(weights)
    in_specs = [pl.BlockSpec((tm, K), lambda n, m: (m, 0))]
    args = [h]
    vmem = 2 * _nbytes((tm, K), h.dtype)
    for w, layer, off in weights:
        assert w.shape[1] == K and off % tn == 0
        in_specs.append(pl.BlockSpec((None, K, tn), lambda n, m, l=layer, o=off // tn: (l, 0, n + o)))
        args.append(w)
        vmem += 2 * _nbytes((K, tn), w.dtype) + _nbytes((K, tn), BF16)
    for p, off, kind in params:
        if kind == "col":
            in_specs.append(pl.BlockSpec((p.shape[0], tn), lambda n, m, o=off // tn: (0, n + o)))
            vmem += 2 * _nbytes((SUBLANES, tn), F32)
        else:
            in_specs.append(pl.BlockSpec((tm, tn), lambda n, m, o=off // tn: (m, n + o)))
            vmem += 2 * _nbytes((tm, tn), p.dtype)
        args.append(p)
    scratch = [pltpu.VMEM((n_w, K, tn), BF16)]
    if n_taps:
        scratch.append(pltpu.VMEM((tm + SUBLANES, tn), F32))
        vmem += _nbytes((tm + SUBLANES, tn), F32)
    vmem += 2 * _nbytes((tm, tn), out_dtype) + n_w * _nbytes((tm, tn), F32)
    kern = functools.partial(_proj_kernel, n_w=n_w, n_p=len(params), mode=mode, tm=tm, tn=tn,
                             n_taps=n_taps, row_block=row_block, row_chunk=row_chunk,
                             q_tiles=q_tiles)
    return pl.pallas_call(
        kern,
        grid=(n_out // tn, S // tm),
        in_specs=in_specs,
        out_specs=pl.BlockSpec((tm, tn), lambda n, m: (m, n)),
        out_shape=jax.ShapeDtypeStruct((S, n_out), out_dtype),
        scratch_shapes=scratch,
        compiler_params=_cparams(2, vmem),
        name="proj_" + mode,
    )(*args)


def _down_kernel(a_ref, w_ref, x_ref, o_ref):
    o_ref[...] = x_ref[...] + jnp.dot(a_ref[...], w_ref[...], preferred_element_type=F32)


def down_residual(a, w_bf16, layer, x, *, tm=512, tn=256):
    S, K = a.shape
    N = w_bf16.shape[2]
    tm = min(tm, S)
    vmem = 2 * (_nbytes((tm, K), a.dtype) + _nbytes((K, tn), BF16) + 2 * _nbytes((tm, tn), F32))
    vmem += _nbytes((tm, tn), F32)
    return pl.pallas_call(
        _down_kernel,
        grid=(S // tm, N // tn),
        in_specs=[pl.BlockSpec((tm, K), lambda m, n: (m, 0)),
                  pl.BlockSpec((None, K, tn), lambda m, n: (layer, 0, n)),
                  pl.BlockSpec((tm, tn), lambda m, n: (m, n))],
        out_specs=pl.BlockSpec((tm, tn), lambda m, n: (m, n)),
        out_shape=jax.ShapeDtypeStruct((S, N), F32),
        compiler_params=_cparams(2, vmem),
        name="down_residual",
    )(a, w_bf16, x)


def _swa_kernel(sink_ref, q_ref, k_ref, v_ref, cos_ref, sin_ref, o_ref, kprev_ref, vprev_ref,
                *, n_kv):
    b = pl.program_id(0)
    D = ATT_HEAD_DIM
    blk = q_ref.shape[0]

    @pl.when(b == 0)
    def _():
        kprev_ref[...] = jnp.zeros(kprev_ref.shape, F32)
        vprev_ref[...] = jnp.zeros(vprev_ref.shape, F32)

    cos = cos_ref[...]
    sin = sin_ref[...]

    def rope(x):
        return x * cos + pltpu.roll(x, D // 2, 1) * sin

    qi = lax.broadcasted_iota(jnp.int32, (blk, 2 * blk), 0)
    kj = lax.broadcasted_iota(jnp.int32, (blk, 2 * blk), 1)
    rel = blk + qi - kj
    valid = (rel >= 0) & (rel < WINDOW) & ((kj >= blk) | (b > 0))
    scale = D ** -0.5

    for g in range(n_kv):
        kc = rope(k_ref[:, g * D:(g + 1) * D])
        vc = v_ref[:, g * D:(g + 1) * D]
        kband = jnp.concatenate([kprev_ref[:, g * D:(g + 1) * D], kc], axis=0).astype(BF16)
        vband = jnp.concatenate([vprev_ref[:, g * D:(g + 1) * D], vc], axis=0).astype(BF16)
        for gg in range(ATT_GROUP):
            hh = g * ATT_GROUP + gg
            qh = rope(q_ref[:, hh * D:(hh + 1) * D])
            s = _mm_nt(qh, kband) * scale
            s = jnp.where(valid, s, -jnp.inf)
            sink = sink_ref[hh]
            mx = jnp.maximum(jnp.max(s, axis=-1, keepdims=True), sink)
            p = jnp.exp(s - mx)
            denom = jnp.sum(p, axis=-1, keepdims=True) + jnp.exp(sink - mx)
            o = _mm(p / denom, vband)
            o_ref[:, hh * D:(hh + 1) * D] = o.astype(o_ref.dtype)
        kprev_ref[:, g * D:(g + 1) * D] = kc
        vprev_ref[:, g * D:(g + 1) * D] = vc


def swa_attention(qkv, sinks, cos_full, sin_signed, n_heads, n_kv):
    S = qkv.shape[0]
    D = ATT_HEAD_DIM
    blk = WINDOW
    qw, kw = n_heads * D, n_kv * D
    vmem = 2 * (_nbytes((blk, qw), F32) + 2 * _nbytes((blk, kw), F32) + 2 * _nbytes((blk, D), F32)
                + _nbytes((blk, qw), BF16)) + 2 * _nbytes((blk, kw), F32)
    return pl.pallas_call(
        functools.partial(_swa_kernel, n_kv=n_kv),
        grid=(S // blk,),
        in_specs=[pl.BlockSpec(memory_space=pltpu.SMEM),
                  pl.BlockSpec((blk, qw), lambda b: (b, 0)),
                  pl.BlockSpec((blk, kw), lambda b: (b, qw // kw)),
                  pl.BlockSpec((blk, kw), lambda b: (b, qw // kw + 1)),
                  pl.BlockSpec((blk, D), lambda b: (b, 0)),
                  pl.BlockSpec((blk, D), lambda b: (b, 0))],
        out_specs=pl.BlockSpec((blk, qw), lambda b: (b, 0)),
        out_shape=jax.ShapeDtypeStruct((S, qw), BF16),
        scratch_shapes=[pltpu.VMEM((blk, kw), F32), pltpu.VMEM((blk, kw), F32)],
        compiler_params=_cparams(1, vmem),
        name="swa_attention",
    )(sinks, qkv, qkv, qkv, cos_full, sin_signed)


def _delta_kernel(q_ref, k_ref, v_ref, z_ref, ba_ref, prm_ref, nw_ref, o_ref, st_ref,
                  *, n_v, rep, head_group):
    c = pl.program_id(0)
    C = q_ref.shape[0]
    D = GDN_HEAD_DIM

    @pl.when(c == 0)
    def _():
        st_ref[...] = jnp.zeros(st_ref.shape, F32)

    ba = ba_ref[...]
    beta_t = jax.nn.sigmoid(ba)
    a_sh = ba + prm_ref[1:2, :]
    softplus = jnp.maximum(a_sh, 0.0) + jnp.log(1.0 + jnp.exp(-jnp.abs(a_sh)))
    g_t = -jnp.exp(prm_ref[0:1, :]) * softplus
    row = lax.broadcasted_iota(jnp.int32, (C, LANES), 0)
    gc = g_t
    s = 1
    while s < C:
        gc = gc + jnp.where(row >= s, pltpu.roll(gc, s, 0), 0.0)
        s *= 2
    glast = gc[C - 1:C, :]
    e_gc = jnp.exp(gc)
    e_rest = jnp.exp(glast - gc)
    e_last = jnp.exp(glast)
    gc_t = gc.T

    ii = lax.broadcasted_iota(jnp.int32, (C, C), 0)
    jj = lax.broadcasted_iota(jnp.int32, (C, C), 1)
    lower = ii >= jj
    strict = ii > jj
    same_blk = (ii // GDN_SOLVE_BLOCK) == (jj // GDN_SOLVE_BLOCK)
    nw = nw_ref[...]

    for g0 in range(0, n_v, head_group):
        hs = list(range(g0, g0 + head_group))
        js = sorted({i // rep for i in hs})
        kh = {j: k_ref[:, j * D:(j + 1) * D] for j in js}
        qh = {j: q_ref[:, j * D:(j + 1) * D] for j in js}
        kk = {j: _mm_nt(kh[j], kh[j]) for j in js}
        qk = {j: _mm_nt(qh[j], kh[j]) for j in js}
        st = {i: st_ref[i] for i in hs}
        og, attn, a_o, tp, pw, rhs = {}, {}, {}, {}, {}, {}
        for i in hs:
            j, gi = i // rep, n_v + i
            bcol = beta_t[:, i:i + 1]
            dec = jnp.exp(jnp.where(lower, gc[:, gi:gi + 1] - gc_t[gi:gi + 1, :], -jnp.inf))
            a_kk = jnp.where(strict, bcol * kk[j] * dec, 0.0)
            attn[i] = qk[j] * dec
            og[i] = _mm(qh[j] * e_gc[:, gi:gi + 1], st[i])
            a_d = jnp.where(same_blk, a_kk, 0.0)
            a_o[i] = a_kk - a_d
            tp[i] = -a_d
            pw[i] = tp[i]
            kb = kh[j] * bcol
            rhs[i] = jnp.concatenate([v_ref[:, i * D:(i + 1) * D] * bcol, kb * e_gc[:, gi:gi + 1]],
                                     axis=1)
        n_sq = int(math.log2(GDN_SOLVE_BLOCK)) - 1
        for i in hs:
            pw[i] = _mm(pw[i], pw[i])
        for s in range(n_sq):
            for i in hs:
                tp[i] = tp[i] + pw[i] + _mm(tp[i], pw[i])
            if s + 1 < n_sq:
                for i in hs:
                    pw[i] = _mm(pw[i], pw[i])
        nn = {i: a_o[i] + _mm(tp[i], a_o[i]) for i in hs}
        y = {i: rhs[i] + _mm(tp[i], rhs[i]) for i in hs}
        n2 = {i: _mm(nn[i], nn[i]) for i in hs}
        zz = {i: y[i] + _mm(n2[i], y[i]) for i in hs}
        x = {i: zz[i] - _mm(nn[i], zz[i]) for i in hs}
        v_new = {i: x[i][:, :D] - _mm(x[i][:, D:], st[i]) for i in hs}
        for i in hs:
            j, gi = i // rep, n_v + i
            out = og[i] + _mm(attn[i], v_new[i])
            kd = kh[j] * e_rest[:, gi:gi + 1]
            st_ref[i] = st[i] * e_last[:, gi:gi + 1] + _mm(kd.T, v_new[i])
            ms = jnp.mean(out * out, axis=-1, keepdims=True)
            zh = z_ref[:, i * D:(i + 1) * D]
            o_ref[:, i * D:(i + 1) * D] = (out * lax.rsqrt(ms + NORM_EPS) * nw * _silu(zh)).astype(o_ref.dtype)


def gated_delta(qk, v, z, ba, prm, norm_w, n_qk, n_v):
    S = qk.shape[0]
    D = GDN_HEAD_DIM
    C = GDN_CHUNK
    qw, vw = n_qk * D, n_v * D
    assert GDN_SOLVE_BLOCK * 4 == C
    vmem = 2 * (2 * _nbytes((C, qw), F32) + 2 * _nbytes((C, vw), F32) + _nbytes((C, vw), BF16))
    vmem += _nbytes((n_v, D, D), F32)
    return pl.pallas_call(
        functools.partial(_delta_kernel, n_v=n_v, rep=n_v // n_qk,
                          head_group=math.gcd(n_v, GDN_HEAD_GROUP)),
        grid=(S // C,),
        in_specs=[pl.BlockSpec((C, qw), lambda c: (c, 0)),
                  pl.BlockSpec((C, qw), lambda c: (c, 1)),
                  pl.BlockSpec((C, vw), lambda c: (c, 0)),
                  pl.BlockSpec((C, vw), lambda c: (c, 0)),
                  pl.BlockSpec((C, LANES), lambda c: (c, 0)),
                  pl.BlockSpec((2, LANES), lambda c: (0, 0)),
                  pl.BlockSpec((1, D), lambda c: (0, 0))],
        out_specs=pl.BlockSpec((C, vw), lambda c: (c, 0)),
        out_shape=jax.ShapeDtypeStruct((S, vw), BF16),
        scratch_shapes=[pltpu.VMEM((n_v, D, D), F32)],
        compiler_params=_cparams(1, vmem),
        name="gated_delta",
    )(qk, qk, v, z, ba, prm, norm_w.reshape(1, D))


CONF_HIST = 32


def _conf_kernel(u_ref, w_ref, bdw_ref, g_ref, b_ref, o_ref, ext_ref, y_ref, *, tm, cw, n_c,
                 row_chunk):
    m = pl.program_id(0)
    c = pl.program_id(1)

    @pl.when(m == 0)
    def _():
        ext_ref[c, 0:CONF_HIST, :] = jnp.zeros((CONF_HIST, cw), F32)

    ext_ref[c, CONF_HIST:CONF_HIST + tm, :] = u_ref[...]
    bias = bdw_ref[...]
    for r in range(0, tm, row_chunk):
        y = None
        for k in range(CONF_KERNEL):
            off = CONF_HIST - (CONF_KERNEL - 1 - k) + r
            term = ext_ref[c, off:off + row_chunk, :] * w_ref[k:k + 1, :]
            y = term if y is None else y + term
        y_ref[c, r:r + row_chunk, :] = y + bias
    ext_ref[c, 0:CONF_HIST, :] = ext_ref[c, tm:tm + CONF_HIST, :]

    @pl.when(c == n_c - 1)
    def _():
        d = n_c * cw
        tot = jnp.zeros((tm, 1), F32)
        for j in range(n_c):
            tot = tot + jnp.sum(y_ref[j], axis=-1, keepdims=True)
        mean = tot / d
        var = jnp.zeros((tm, 1), F32)
        for j in range(n_c):
            yc = y_ref[j] - mean
            var = var + jnp.sum(yc * yc, axis=-1, keepdims=True)
        inv = lax.rsqrt(var / d + LN_EPS)
        for j in range(n_c):
            zz = (y_ref[j] - mean) * inv * g_ref[:, j * cw:(j + 1) * cw] + b_ref[:, j * cw:(j + 1) * cw]
            o_ref[:, j * cw:(j + 1) * cw] = _silu(zz).astype(o_ref.dtype)


def conformer_mid(u, w_dw, b_dw, ln_g, ln_b, *, tm=256, cw=512, row_chunk=32):
    S, D = u.shape
    tm, cw = min(tm, S), min(cw, D)
    n_c = D // cw
    vmem = 2 * (_nbytes((tm, cw), F32) + _nbytes((CONF_HIST, cw), F32) + _nbytes((tm, D), BF16))
    vmem += _nbytes((n_c, tm + CONF_HIST, cw), F32) + 3 * _nbytes((n_c, tm, cw), F32)
    kern = functools.partial(_conf_kernel, tm=tm, cw=cw, n_c=n_c, row_chunk=row_chunk)
    return pl.pallas_call(
        kern,
        grid=(S // tm, n_c),
        in_specs=[pl.BlockSpec((tm, cw), lambda m, c: (m, c)),
                  pl.BlockSpec((CONF_KERNEL, cw), lambda m, c: (0, c)),
                  pl.BlockSpec((1, cw), lambda m, c: (0, c)),
                  pl.BlockSpec((1, D), lambda m, c: (0, 0)),
                  pl.BlockSpec((1, D), lambda m, c: (0, 0))],
        out_specs=pl.BlockSpec((tm, D), lambda m, c: (m, 0)),
        out_shape=jax.ShapeDtypeStruct((S, D), BF16),
        scratch_shapes=[pltpu.VMEM((n_c, tm + CONF_HIST, cw), F32),
                        pltpu.VMEM((n_c, tm, cw), F32)],
        compiler_params=_cparams(2, vmem),
        name="conformer_mid",
    )(u, w_dw, b_dw.reshape(1, D), ln_g.reshape(1, D), ln_b.reshape(1, D))


def _rope_tables(S):
    half = ATT_HEAD_DIM // 2
    inv = jnp.power(ROPE_THETA, -jnp.arange(half, dtype=F32) / half)
    ang = jnp.arange(S).astype(F32)[:, None] * inv[None, :]
    cos, sin = jnp.cos(ang), jnp.sin(ang)
    return jnp.concatenate([cos, cos], axis=1), jnp.concatenate([-sin, sin], axis=1)


def _wide_tile(n):
    return 512 if n % 512 == 0 else 256


def _mixer_a(h, x, w_qkv, w_o, sinks, j):
    S, Dm = x.shape
    n_heads = w_o.shape[1] // ATT_HEAD_DIM
    n_kv = n_heads // ATT_GROUP
    n_qkv = w_qkv.shape[2]
    qkv = proj(h, [(w_qkv, j, 0)], [], "plain", n_qkv, F32, tn=_wide_tile(n_qkv))
    cos_full, sin_signed = _rope_tables(S)
    o = swa_attention(qkv, sinks[j], cos_full, sin_signed, n_heads, n_kv)
    return proj(o, [(w_o, j, 0)], [(x, 0, "tile")], "residual", Dm, F32, tn=_wide_tile(Dm))


def _mixer_b(h, x, w_in, conv_w, a_log, dt_bias, norm_w, w_o, j):
    S, Dm = x.shape
    D = GDN_HEAD_DIM
    conv_w, a_log, dt_bias, norm_w = conv_w[j], a_log[j], dt_bias[j], norm_w[j]
    n_v = a_log.shape[0]
    val_dim = n_v * D
    key_dim = (conv_w.shape[1] - val_dim) // 2
    n_qk = key_dim // D
    n_qkv = 2 * key_dim + val_dim
    tn = _wide_tile(key_dim)
    n_taps = conv_w.shape[0]
    qk = proj(h, [(w_in, j, 0)], [(conv_w, 0, "col")], "conv_silu_l2", 2 * key_dim, F32, tn=tn,
              n_taps=n_taps, q_tiles=key_dim // tn)
    v = proj(h, [(w_in, j, 2 * key_dim)], [(conv_w, 2 * key_dim, "col")], "conv_silu", val_dim, F32,
             tn=tn, n_taps=n_taps)
    z = proj(h, [(w_in, j, n_qkv)], [], "plain", val_dim, F32, tn=tn)
    w_ba = jnp.pad(w_in[j][:, n_qkv + val_dim:], ((0, 0), (0, LANES - 2 * n_v)))
    ba = proj(h, [(w_ba[None], 0, 0)], [], "plain", LANES, F32, tn=LANES)
    prm = jnp.zeros((2, LANES), F32)
    prm = prm.at[0, n_v:2 * n_v].set(a_log).at[1, n_v:2 * n_v].set(dt_bias)
    o = gated_delta(qk, v, z, ba, prm, norm_w, n_qk, n_v)
    return proj(o, [(w_o, j, 0)], [(x, 0, "tile")], "residual", Dm, F32, tn=_wide_tile(Dm))


def _mixer_c(h, x, w_pw1, b_pw1, w_dw, b_dw, ln_g, ln_b, w_pw2, b_pw2, j):
    S, Dm = x.shape
    b1 = b_pw1[j].reshape(1, -1)
    u = proj(h, [(w_pw1, j, 0), (w_pw1, j, Dm)], [(b1, 0, "col"), (b1, Dm, "col")], "glu_bias", Dm,
             F32, tn=256)
    mid = conformer_mid(u, w_dw[j], b_dw[j], ln_g[j], ln_b[j])
    return proj(mid, [(w_pw2, j, 0)], [(x, 0, "tile"), (b_pw2[j].reshape(1, -1), 0, "col")],
                "residual", Dm, F32, tn=_wide_tile(Dm))


def _mixer_d(h, x, w_in, w_conv, w_out, j):
    S, Dm = x.shape
    mid = proj(h, [(w_in, j, 0), (w_in, j, Dm), (w_in, j, 2 * Dm)], [(w_conv[j], 0, "col")],
               "short_conv", Dm, BF16, tn=256, n_taps=w_conv.shape[1])
    return proj(mid, [(w_out, j, 0)], [(x, 0, "tile")], "residual", Dm, F32, tn=_wide_tile(Dm))


def _ffn(h, x, w_gate, w_up, w_conv, b_conv, w_down, i):
    d_ff = w_gate.shape[2]
    mid = proj(h, [(w_gate, i, 0), (w_up, i, 0)],
               [(w_conv[i], 0, "col"), (b_conv[i].reshape(1, -1), 0, "col")],
               "conv_glu", d_ff, BF16, tn=256, n_taps=w_conv.shape[1])
    return down_residual(mid, w_down.astype(BF16), i, x)


def kernel(x, mix_norm, ffn_norm, final_norm, a_w_qkv, a_w_o, a_sinks, b_w_in, b_conv, b_a_log, b_dt_bias, b_norm, b_w_o, c_w_pw1, c_b_pw1, c_w_dw, c_b_dw, c_ln_g, c_ln_b, c_w_pw2, c_b_pw2, d_w_in, d_w_conv, d_w_out, f_w_gate, f_w_up, f_w_conv, f_b_conv, f_w_down):
    B, S, Dm = x.shape
    depth = mix_norm.shape[0]
    outs = []
    for bi in range(B):
        xb = x[bi]
        for i in range(depth):
            kind, j = i % 4, i // 4
            h = rmsnorm(xb, mix_norm[i], BF16)
            if kind == 0:
                xb = _mixer_a(h, xb, a_w_qkv, a_w_o, a_sinks, j)
            elif kind == 1:
                xb = _mixer_b(h, xb, b_w_in, b_conv, b_a_log, b_dt_bias, b_norm, b_w_o, j)
            elif kind == 2:
                xb = _mixer_c(h, xb, c_w_pw1, c_b_pw1, c_w_dw, c_b_dw, c_ln_g, c_ln_b, c_w_pw2,
                              c_b_pw2, j)
            else:
                xb = _mixer_d(h, xb, d_w_in, d_w_conv, d_w_out, j)
            h = rmsnorm(xb, ffn_norm[i], BF16)
            xb = _ffn(h, xb, f_w_gate, f_w_up, f_w_conv, f_b_conv, f_w_down, i)
        outs.append(rmsnorm(xb, final_norm, F32))
    return jnp.stack(outs, axis=0)
```

```python
import functools
import math

import jax
import jax.numpy as jnp
from jax import lax
from jax.experimental import pallas as pl
from jax.experimental.pallas import tpu as pltpu

F32 = jnp.float32
BF16 = jnp.bfloat16

NORM_EPS = 1e-6
LN_EPS = 1e-5
L2_EPS = 1e-6
ROPE_THETA = 10000.0
ATT_HEAD_DIM = 128
ATT_GROUP = 4
ATT_KV_GROUP = 2
WINDOW = 128
GDN_HEAD_DIM = 128
GDN_CHUNK = 64
GDN_SOLVE_BLOCK = 16
GDN_HEAD_GROUP = 16
CONF_KERNEL = 31

LANES = 128
SUBLANES = 8
VMEM_CAP_BYTES = 58 * 1024 * 1024
VMEM_SLACK_BYTES = 6 * 1024 * 1024


def _cparams(n_axes, vmem_bytes):
    limit = int(min(VMEM_CAP_BYTES, max(vmem_bytes + VMEM_SLACK_BYTES, 16 * 1024 * 1024)))
    return pltpu.CompilerParams(dimension_semantics=("arbitrary",) * n_axes, vmem_limit_bytes=limit)


def _nbytes(shape, dtype):
    return math.prod(shape) * jnp.dtype(dtype).itemsize


def _silu(x):
    return x * jax.nn.sigmoid(x)


def _mm(a, b):
    return jnp.dot(a.astype(BF16), b.astype(BF16), preferred_element_type=F32)


def _mm_nt(a, b):
    return lax.dot_general(a.astype(BF16), b.astype(BF16), (((1,), (1,)), ((), ())),
                           preferred_element_type=F32)


def _rmsnorm_kernel(x_ref, g_ref, o_ref):
    x = x_ref[...]
    ms = jnp.mean(x * x, axis=-1, keepdims=True)
    o_ref[...] = (x * lax.rsqrt(ms + NORM_EPS) * g_ref[...]).astype(o_ref.dtype)


def rmsnorm(x, g, out_dtype, tm=512):
    S, D = x.shape
    tm = min(tm, S)
    vmem = 2 * (_nbytes((tm, D), F32) + _nbytes((tm, D), out_dtype)) + _nbytes((tm, D), F32)
    return pl.pallas_call(
        _rmsnorm_kernel,
        grid=(S // tm,),
        in_specs=[pl.BlockSpec((tm, D), lambda m: (m, 0)),
                  pl.BlockSpec((1, D), lambda m: (0, 0))],
        out_specs=pl.BlockSpec((tm, D), lambda m: (m, 0)),
        out_shape=jax.ShapeDtypeStruct((S, D), out_dtype),
        compiler_params=_cparams(1, vmem),
        name="rmsnorm",
    )(x, g.reshape(1, D))


def _gain_rstd_kernel(x_ref, g_ref, xb_ref, rs_ref):
    x = x_ref[...]
    ms = jnp.mean(x * x, axis=-1, keepdims=True)
    xb_ref[...] = (x * g_ref[...]).astype(xb_ref.dtype)
    rs_ref[...] = jnp.broadcast_to(lax.rsqrt(ms + NORM_EPS), rs_ref.shape)


def gain_rstd(x, g, tm=512):
    S, D = x.shape
    tm = min(tm, S)
    vmem = 2 * (_nbytes((tm, D), F32) + _nbytes((tm, D), BF16)) + _nbytes((tm, D), F32)
    return pl.pallas_call(
        _gain_rstd_kernel,
        grid=(S // tm,),
        in_specs=[pl.BlockSpec((tm, D), lambda m: (m, 0)),
                  pl.BlockSpec((1, D), lambda m: (0, 0))],
        out_specs=[pl.BlockSpec((tm, D), lambda m: (m, 0)),
                   pl.BlockSpec((tm, LANES), lambda m: (m, 0))],
        out_shape=[jax.ShapeDtypeStruct((S, D), BF16), jax.ShapeDtypeStruct((S, LANES), F32)],
        compiler_params=_cparams(1, vmem),
        name="gain_rstd",
    )(x, g.reshape(1, D))


def _causal_taps(ext_ref, w_ref, n_taps, r, rows):
    y = None
    for k in range(n_taps):
        off = SUBLANES - (n_taps - 1 - k) + r
        term = ext_ref[off:off + rows, :] * w_ref[k:k + 1, :]
        y = term if y is None else y + term
    return y


def _proj_kernel(*refs, n_w, n_p, mode, tm, tn, n_taps, row_block, row_chunk, q_tiles):
    h_ref, rs_ref = refs[0], refs[1]
    w_refs = refs[2:2 + n_w]
    p_refs = refs[2 + n_w:2 + n_w + n_p]
    o_ref = refs[2 + n_w + n_p]
    wbf_ref = refs[3 + n_w + n_p]
    ext_ref = refs[4 + n_w + n_p] if n_taps else None
    n = pl.program_id(0)
    m = pl.program_id(1)

    @pl.when(m == 0)
    def _():
        for i in range(n_w):
            wbf_ref[:, i * tn:(i + 1) * tn] = w_refs[i][...].astype(BF16)
        if n_taps:
            ext_ref[0:SUBLANES, :] = jnp.zeros((SUBLANES, tn), F32)

    def dots(r0):
        acc = jnp.dot(h_ref[r0:r0 + row_block, :], wbf_ref[...], preferred_element_type=F32)
        rs = rs_ref[r0:r0 + row_block, :]
        rs = jnp.concatenate([rs] * (tn // LANES), axis=1)
        return [acc[:, i * tn:(i + 1) * tn] * rs for i in range(n_w)]

    def to_ext(r0, val):
        ext_ref[SUBLANES + r0:SUBLANES + r0 + row_block, :] = val

    def chunks(r0):
        return [(r, r - r0) for r in range(r0, r0 + row_block, row_chunk)]

    if mode == "plain":
        def matmul(r0):
            return dots(r0)[0]

        def finish(r0, acc):
            o_ref[r0:r0 + row_block, :] = acc.astype(o_ref.dtype)
    elif mode == "glu_bias":
        def matmul(r0):
            return dots(r0)

        def finish(r0, accs):
            val = accs[0] + p_refs[0][...]
            gate = accs[1] + p_refs[1][...]
            o_ref[r0:r0 + row_block, :] = (val * jax.nn.sigmoid(gate)).astype(o_ref.dtype)
    elif mode == "conv_silu":
        def matmul(r0):
            to_ext(r0, dots(r0)[0])

        def finish(r0, _):
            for r, _ in chunks(r0):
                y = _causal_taps(ext_ref, p_refs[0], n_taps, r, row_chunk)
                o_ref[r:r + row_chunk, :] = _silu(y).astype(o_ref.dtype)
    elif mode == "conv_silu_l2":
        scale = jnp.where(n < q_tiles, GDN_HEAD_DIM ** -0.5, 1.0).astype(F32)

        def matmul(r0):
            to_ext(r0, dots(r0)[0])

        def finish(r0, _):
            for r, _ in chunks(r0):
                y = _silu(_causal_taps(ext_ref, p_refs[0], n_taps, r, row_chunk))
                for j in range(0, tn, GDN_HEAD_DIM):
                    yh = y[:, j:j + GDN_HEAD_DIM]
                    inv = lax.rsqrt(jnp.sum(yh * yh, axis=-1, keepdims=True) + L2_EPS)
                    o_ref[r:r + row_chunk, j:j + GDN_HEAD_DIM] = (yh * inv * scale).astype(o_ref.dtype)
    elif mode == "short_conv":
        def matmul(r0):
            bg, cg, xin = dots(r0)
            to_ext(r0, cg * xin)
            return bg

        def finish(r0, bg):
            for r, rr in chunks(r0):
                y = _causal_taps(ext_ref, p_refs[0], n_taps, r, row_chunk)
                o_ref[r:r + row_chunk, :] = (bg[rr:rr + row_chunk, :] * y).astype(o_ref.dtype)
    elif mode == "conv_glu":
        def matmul(r0):
            gate, up = dots(r0)
            to_ext(r0, gate)
            return up

        def finish(r0, up):
            bias = p_refs[1][...]
            for r, rr in chunks(r0):
                y = _causal_taps(ext_ref, p_refs[0], n_taps, r, row_chunk)
                o_ref[r:r + row_chunk, :] = (_silu(y + bias) * up[rr:rr + row_chunk, :]).astype(o_ref.dtype)
    else:
        raise ValueError(mode)

    pending = None
    for r0 in range(0, tm, row_block):
        acc = matmul(r0)
        if pending is not None:
            finish(*pending)
        pending = (r0, acc)
    finish(*pending)
    if n_taps:
        ext_ref[0:SUBLANES, :] = ext_ref[tm:tm + SUBLANES, :]


def proj(hn, weights, params, mode, n_out, out_dtype, *, tn, tm=1024, n_taps=0, q_tiles=0,
         row_block=1024, row_chunk=128):
    h, rstd = hn
    S, K = h.shape
    tm = min(tm, S)
    row_block = min(row_block, tm)
    assert S % tm == 0 and n_out % tn == 0 and tm % row_block == 0 and row_block % row_chunk == 0
    n_w = len(weights)
    in_specs = [pl.BlockSpec((tm, K), lambda n, m: (m, 0)),
                pl.BlockSpec((tm, LANES), lambda n, m: (m, 0))]
    args = [h, rstd]
    vmem = 2 * (_nbytes((tm, K), h.dtype) + _nbytes((tm, LANES), F32))
    for w, layer, off in weights:
        assert w.shape[1] == K and off % tn == 0
        in_specs.append(pl.BlockSpec((None, K, tn), lambda n, m, l=layer, o=off // tn: (l, 0, n + o)))
        args.append(w)
        vmem += 2 * _nbytes((K, tn), w.dtype) + _nbytes((K, tn), BF16)
    for p, off in params:
        in_specs.append(pl.BlockSpec((p.shape[0], tn), lambda n, m, o=off // tn: (0, n + o)))
        vmem += 2 * _nbytes((SUBLANES, tn), F32)
        args.append(p)
    scratch = [pltpu.VMEM((K, n_w * tn), BF16)]
    if n_taps:
        scratch.append(pltpu.VMEM((tm + SUBLANES, tn), F32))
        vmem += _nbytes((tm + SUBLANES, tn), F32)
    vmem += 2 * _nbytes((tm, tn), out_dtype) + n_w * _nbytes((tm, tn), F32)
    kern = functools.partial(_proj_kernel, n_w=n_w, n_p=len(params), mode=mode, tm=tm, tn=tn,
                             n_taps=n_taps, row_block=row_block, row_chunk=row_chunk,
                             q_tiles=q_tiles)
    return pl.pallas_call(
        kern,
        grid=(n_out // tn, S // tm),
        in_specs=in_specs,
        out_specs=pl.BlockSpec((tm, tn), lambda n, m: (m, n)),
        out_shape=jax.ShapeDtypeStruct((S, n_out), out_dtype),
        scratch_shapes=scratch,
        compiler_params=_cparams(2, vmem),
        name="proj_" + mode,
    )(*args)


def _down_kernel(*refs, has_bias, n_total):
    a_ref, w_ref, x_ref, g_ref = refs[:4]
    b_ref = refs[4] if has_bias else None
    o_ref, xb_ref, rs_ref = refs[-3:]
    n = pl.program_id(1)
    acc = jnp.dot(a_ref[...], w_ref[...], preferred_element_type=F32)
    if has_bias:
        acc = acc + b_ref[...]
    xn = x_ref[...] + acc
    o_ref[...] = xn
    xb_ref[...] = (xn * g_ref[...]).astype(xb_ref.dtype)
    sq = xn * xn
    part = sq[:, 0:LANES]
    for j in range(LANES, sq.shape[1], LANES):
        part = part + sq[:, j:j + LANES]

    @pl.when(n == 0)
    def _():
        rs_ref[...] = part

    @pl.when(n > 0)
    def _():
        rs_ref[...] = rs_ref[...] + part

    @pl.when(n == pl.num_programs(1) - 1)
    def _():
        ms = jnp.sum(rs_ref[...], axis=-1, keepdims=True) / n_total
        rs_ref[...] = jnp.broadcast_to(lax.rsqrt(ms + NORM_EPS), rs_ref.shape)


def down_residual(a, w_bf16, layer, x, gain, bias=None, *, tm=512, tn=512):
    S, K = a.shape
    N = w_bf16.shape[2]
    tm = min(tm, S)
    vmem = 2 * (_nbytes((tm, K), a.dtype) + _nbytes((K, tn), BF16) + 2 * _nbytes((tm, tn), F32)
                + _nbytes((tm, tn), BF16) + _nbytes((tm, LANES), F32))
    vmem += 2 * _nbytes((tm, tn), F32)
    in_specs = [pl.BlockSpec((tm, K), lambda m, n: (m, 0)),
                pl.BlockSpec((None, K, tn), lambda m, n: (layer, 0, n)),
                pl.BlockSpec((tm, tn), lambda m, n: (m, n)),
                pl.BlockSpec((1, tn), lambda m, n: (0, n))]
    args = [a, w_bf16, x, gain.reshape(1, N)]
    if bias is not None:
        in_specs.append(pl.BlockSpec((1, tn), lambda m, n: (0, n)))
        args.append(bias.reshape(1, N))
    x_new, xb, rstd = pl.pallas_call(
        functools.partial(_down_kernel, has_bias=bias is not None, n_total=N),
        grid=(S // tm, N // tn),
        in_specs=in_specs,
        out_specs=[pl.BlockSpec((tm, tn), lambda m, n: (m, n)),
                   pl.BlockSpec((tm, tn), lambda m, n: (m, n)),
                   pl.BlockSpec((tm, LANES), lambda m, n: (m, 0))],
        out_shape=[jax.ShapeDtypeStruct((S, N), F32), jax.ShapeDtypeStruct((S, N), BF16),
                   jax.ShapeDtypeStruct((S, LANES), F32)],
        compiler_params=_cparams(2, vmem),
        name="down_residual",
    )(*args)
    return x_new, (xb, rstd)


def _swa_kernel(sink_ref, q_ref, k_ref, v_ref, cos_ref, sin_ref, o_ref, kprev_ref, vprev_ref,
                *, n_kv, kv_group):
    b = pl.program_id(0)
    D = ATT_HEAD_DIM
    blk = q_ref.shape[0]

    @pl.when(b == 0)
    def _():
        kprev_ref[...] = jnp.zeros(kprev_ref.shape, F32)
        vprev_ref[...] = jnp.zeros(vprev_ref.shape, F32)

    cos = cos_ref[...]
    sin = sin_ref[...]

    def rope(x):
        return x * cos + pltpu.roll(x, D // 2, 1) * sin

    qi = lax.broadcasted_iota(jnp.int32, (blk, 2 * blk), 0)
    kj = lax.broadcasted_iota(jnp.int32, (blk, 2 * blk), 1)
    rel = blk + qi - kj
    valid = (rel >= 0) & (rel < WINDOW) & ((kj >= blk) | (b > 0))
    scale = D ** -0.5

    for g0 in range(0, n_kv, kv_group):
        gs = range(g0, g0 + kv_group)
        kband, vband, s, p = {}, {}, {}, {}
        for g in gs:
            kc = rope(k_ref[:, g * D:(g + 1) * D])
            vc = v_ref[:, g * D:(g + 1) * D]
            kband[g] = jnp.concatenate([kprev_ref[:, g * D:(g + 1) * D], kc], axis=0).astype(BF16)
            vband[g] = jnp.concatenate([vprev_ref[:, g * D:(g + 1) * D], vc], axis=0).astype(BF16)
            kprev_ref[:, g * D:(g + 1) * D] = kc
            vprev_ref[:, g * D:(g + 1) * D] = vc
        hs = [(g, g * ATT_GROUP + gg) for g in gs for gg in range(ATT_GROUP)]
        for g, hh in hs:
            qh = rope(q_ref[:, hh * D:(hh + 1) * D])
            s[hh] = _mm_nt(qh, kband[g]) * scale
        for g, hh in hs:
            sh = jnp.where(valid, s[hh], -jnp.inf)
            sink = sink_ref[hh]
            mx = jnp.maximum(jnp.max(sh, axis=-1, keepdims=True), sink)
            e = jnp.exp(sh - mx)
            denom = jnp.sum(e, axis=-1, keepdims=True) + jnp.exp(sink - mx)
            p[hh] = e / denom
        for g, hh in hs:
            o_ref[:, hh * D:(hh + 1) * D] = _mm(p[hh], vband[g]).astype(o_ref.dtype)


def swa_attention(qkv, sinks, cos_full, sin_signed, n_heads, n_kv):
    S = qkv.shape[0]
    D = ATT_HEAD_DIM
    blk = WINDOW
    qw, kw = n_heads * D, n_kv * D
    vmem = 2 * (_nbytes((blk, qw), F32) + 2 * _nbytes((blk, kw), F32) + 2 * _nbytes((blk, D), F32)
                + _nbytes((blk, qw), BF16)) + 2 * _nbytes((blk, kw), F32)
    return pl.pallas_call(
        functools.partial(_swa_kernel, n_kv=n_kv, kv_group=math.gcd(n_kv, ATT_KV_GROUP)),
        grid=(S // blk,),
        in_specs=[pl.BlockSpec(memory_space=pltpu.SMEM),
                  pl.BlockSpec((blk, qw), lambda b: (b, 0)),
                  pl.BlockSpec((blk, kw), lambda b: (b, qw // kw)),
                  pl.BlockSpec((blk, kw), lambda b: (b, qw // kw + 1)),
                  pl.BlockSpec((blk, D), lambda b: (b, 0)),
                  pl.BlockSpec((blk, D), lambda b: (b, 0))],
        out_specs=pl.BlockSpec((blk, qw), lambda b: (b, 0)),
        out_shape=jax.ShapeDtypeStruct((S, qw), BF16),
        scratch_shapes=[pltpu.VMEM((blk, kw), F32), pltpu.VMEM((blk, kw), F32)],
        compiler_params=_cparams(1, vmem),
        name="swa_attention",
    )(sinks, qkv, qkv, qkv, cos_full, sin_signed)


def _delta_kernel(q_ref, k_ref, v_ref, z_ref, ba_ref, prm_ref, nw_ref, o_ref, st_ref,
                  *, n_v, rep, head_group):
    c = pl.program_id(0)
    C = q_ref.shape[0]
    D = GDN_HEAD_DIM

    @pl.when(c == 0)
    def _():
        st_ref[...] = jnp.zeros(st_ref.shape, F32)

    ba = ba_ref[...]
    beta_t = jax.nn.sigmoid(ba)
    a_sh = ba + prm_ref[1:2, :]
    softplus = jnp.maximum(a_sh, 0.0) + jnp.log(1.0 + jnp.exp(-jnp.abs(a_sh)))
    g_t = -jnp.exp(prm_ref[0:1, :]) * softplus
    row = lax.broadcasted_iota(jnp.int32, (C, LANES), 0)
    gc = g_t
    s = 1
    while s < C:
        gc = gc + jnp.where(row >= s, pltpu.roll(gc, s, 0), 0.0)
        s *= 2
    glast = gc[C - 1:C, :]
    e_gc = jnp.exp(gc)
    e_rest = jnp.exp(glast - gc)
    e_last = jnp.exp(glast)
    gc_t = gc.T

    ii = lax.broadcasted_iota(jnp.int32, (C, C), 0)
    jj = lax.broadcasted_iota(jnp.int32, (C, C), 1)
    lower = ii >= jj
    strict = ii > jj
    same_blk = (ii // GDN_SOLVE_BLOCK) == (jj // GDN_SOLVE_BLOCK)
    nw = nw_ref[...]

    for g0 in range(0, n_v, head_group):
        hs = list(range(g0, g0 + head_group))
        js = sorted({i // rep for i in hs})
        kh = {j: k_ref[:, j * D:(j + 1) * D] for j in js}
        qh = {j: q_ref[:, j * D:(j + 1) * D] for j in js}
        kk = {j: _mm_nt(kh[j], kh[j]) for j in js}
        qk = {j: _mm_nt(qh[j], kh[j]) for j in js}
        st = {i: st_ref[i] for i in hs}
        og, attn, a_o, tp, pw, rhs = {}, {}, {}, {}, {}, {}
        for i in hs:
            j, gi = i // rep, n_v + i
            bcol = beta_t[:, i:i + 1]
            dec = jnp.exp(jnp.where(lower, gc[:, gi:gi + 1] - gc_t[gi:gi + 1, :], -jnp.inf))
            a_kk = jnp.where(strict, bcol * kk[j] * dec, 0.0)
            attn[i] = qk[j] * dec
            og[i] = _mm(qh[j] * e_gc[:, gi:gi + 1], st[i])
            a_d = jnp.where(same_blk, a_kk, 0.0)
            a_o[i] = a_kk - a_d
            tp[i] = -a_d
            pw[i] = tp[i]
            kb = kh[j] * bcol
            rhs[i] = jnp.concatenate([v_ref[:, i * D:(i + 1) * D] * bcol, kb * e_gc[:, gi:gi + 1]],
                                     axis=1)
        n_sq = int(math.log2(GDN_SOLVE_BLOCK)) - 1
        for i in hs:
            pw[i] = _mm(pw[i], pw[i])
        for s in range(n_sq):
            for i in hs:
                tp[i] = tp[i] + pw[i] + _mm(tp[i], pw[i])
            if s + 1 < n_sq:
                for i in hs:
                    pw[i] = _mm(pw[i], pw[i])
        nn = {i: a_o[i] + _mm(tp[i], a_o[i]) for i in hs}
        y = {i: rhs[i] + _mm(tp[i], rhs[i]) for i in hs}
        n2 = {i: _mm(nn[i], nn[i]) for i in hs}
        zz = {i: y[i] + _mm(n2[i], y[i]) for i in hs}
        x = {i: zz[i] - _mm(nn[i], zz[i]) for i in hs}
        v_new = {i: x[i][:, :D] - _mm(x[i][:, D:], st[i]) for i in hs}
        for i in hs:
            j, gi = i // rep, n_v + i
            out = og[i] + _mm(attn[i], v_new[i])
            kd = kh[j] * e_rest[:, gi:gi + 1]
            st_ref[i] = st[i] * e_last[:, gi:gi + 1] + _mm(kd.T, v_new[i])
            ms = jnp.mean(out * out, axis=-1, keepdims=True)
            zh = z_ref[:, i * D:(i + 1) * D]
            o_ref[:, i * D:(i + 1) * D] = (out * lax.rsqrt(ms + NORM_EPS) * nw * _silu(zh)).astype(o_ref.dtype)


def gated_delta(qk, v, z, ba, prm, norm_w, n_qk, n_v):
    S = qk.shape[0]
    D = GDN_HEAD_DIM
    C = GDN_CHUNK
    qw, vw = n_qk * D, n_v * D
    assert GDN_SOLVE_BLOCK * 4 == C
    vmem = 2 * (2 * _nbytes((C, qw), F32) + 2 * _nbytes((C, vw), F32) + _nbytes((C, vw), BF16))
    vmem += _nbytes((n_v, D, D), F32)
    return pl.pallas_call(
        functools.partial(_delta_kernel, n_v=n_v, rep=n_v // n_qk,
                          head_group=math.gcd(n_v, GDN_HEAD_GROUP)),
        grid=(S // C,),
        in_specs=[pl.BlockSpec((C, qw), lambda c: (c, 0)),
                  pl.BlockSpec((C, qw), lambda c: (c, 1)),
                  pl.BlockSpec((C, vw), lambda c: (c, 0)),
                  pl.BlockSpec((C, vw), lambda c: (c, 0)),
                  pl.BlockSpec((C, LANES), lambda c: (c, 0)),
                  pl.BlockSpec((2, LANES), lambda c: (0, 0)),
                  pl.BlockSpec((1, D), lambda c: (0, 0))],
        out_specs=pl.BlockSpec((C, vw), lambda c: (c, 0)),
        out_shape=jax.ShapeDtypeStruct((S, vw), BF16),
        scratch_shapes=[pltpu.VMEM((n_v, D, D), F32)],
        compiler_params=_cparams(1, vmem),
        name="gated_delta",
    )(qk, qk, v, z, ba, prm, norm_w.reshape(1, D))


CONF_HIST = 32


def _conf_kernel(u_ref, w_ref, bdw_ref, g_ref, b_ref, o_ref, ext_ref, sh_ref, y_ref, *, tm, cw,
                 n_c, row_chunk):
    m = pl.program_id(0)
    c = pl.program_id(1)

    @pl.when(m == 0)
    def _():
        ext_ref[c, 0:CONF_HIST, :] = jnp.zeros((CONF_HIST, cw), F32)

    ext_ref[c, CONF_HIST:CONF_HIST + tm, :] = u_ref[...]
    n_rows = CONF_HIST + tm - SUBLANES
    for j in range(1, SUBLANES):
        sh_ref[j - 1, SUBLANES:SUBLANES + n_rows, :] = ext_ref[c, SUBLANES - j:SUBLANES - j + n_rows, :]
    bias = bdw_ref[...]
    for r in range(0, tm, row_chunk):
        y = None
        for k in range(CONF_KERNEL):
            shift = CONF_KERNEL - 1 - k
            a, j = divmod(shift, SUBLANES)
            off = CONF_HIST - SUBLANES * a + r
            if j == 0:
                src = ext_ref[c, off:off + row_chunk, :]
            else:
                src = sh_ref[j - 1, off:off + row_chunk, :]
            term = src * w_ref[k:k + 1, :]
            y = term if y is None else y + term
        y_ref[c, r:r + row_chunk, :] = y + bias
    ext_ref[c, 0:CONF_HIST, :] = ext_ref[c, tm:tm + CONF_HIST, :]

    @pl.when(c == n_c - 1)
    def _():
        d = n_c * cw
        tot = jnp.zeros((tm, 1), F32)
        for j in range(n_c):
            tot = tot + jnp.sum(y_ref[j], axis=-1, keepdims=True)
        mean = tot / d
        var = jnp.zeros((tm, 1), F32)
        for j in range(n_c):
            yc = y_ref[j] - mean
            var = var + jnp.sum(yc * yc, axis=-1, keepdims=True)
        inv = lax.rsqrt(var / d + LN_EPS)
        for j in range(n_c):
            zz = (y_ref[j] - mean) * inv * g_ref[:, j * cw:(j + 1) * cw] + b_ref[:, j * cw:(j + 1) * cw]
            o_ref[:, j * cw:(j + 1) * cw] = _silu(zz).astype(o_ref.dtype)


def conformer_mid(u, w_dw, b_dw, ln_g, ln_b, *, tm=256, cw=512, row_chunk=32):
    S, D = u.shape
    tm, cw = min(tm, S), min(cw, D)
    n_c = D // cw
    vmem = 2 * (_nbytes((tm, cw), F32) + _nbytes((CONF_HIST, cw), F32) + _nbytes((tm, D), BF16))
    vmem += _nbytes((n_c + SUBLANES - 1, tm + CONF_HIST, cw), F32) + 3 * _nbytes((n_c, tm, cw), F32)
    kern = functools.partial(_conf_kernel, tm=tm, cw=cw, n_c=n_c, row_chunk=row_chunk)
    return pl.pallas_call(
        kern,
        grid=(S // tm, n_c),
        in_specs=[pl.BlockSpec((tm, cw), lambda m, c: (m, c)),
                  pl.BlockSpec((CONF_KERNEL, cw), lambda m, c: (0, c)),
                  pl.BlockSpec((1, cw), lambda m, c: (0, c)),
                  pl.BlockSpec((1, D), lambda m, c: (0, 0)),
                  pl.BlockSpec((1, D), lambda m, c: (0, 0))],
        out_specs=pl.BlockSpec((tm, D), lambda m, c: (m, 0)),
        out_shape=jax.ShapeDtypeStruct((S, D), BF16),
        scratch_shapes=[pltpu.VMEM((n_c, tm + CONF_HIST, cw), F32),
                        pltpu.VMEM((SUBLANES - 1, tm + CONF_HIST, cw), F32),
                        pltpu.VMEM((n_c, tm, cw), F32)],
        compiler_params=_cparams(2, vmem),
        name="conformer_mid",
    )(u, w_dw, b_dw.reshape(1, D), ln_g.reshape(1, D), ln_b.reshape(1, D))


def _rope_tables(S):
    half = ATT_HEAD_DIM // 2
    inv = jnp.power(ROPE_THETA, -jnp.arange(half, dtype=F32) / half)
    ang = jnp.arange(S).astype(F32)[:, None] * inv[None, :]
    cos, sin = jnp.cos(ang), jnp.sin(ang)
    return jnp.concatenate([cos, cos], axis=1), jnp.concatenate([-sin, sin], axis=1)


def _wide_tile(n):
    return 512 if n % 512 == 0 else 256


def _out_proj(a, w_o, j, x, gain, bias=None):
    return down_residual(a, w_o.astype(BF16), j, x, gain, bias, tm=1024, tn=_wide_tile(x.shape[1]))


def _mixer_a(hn, x, gain, w_qkv, w_o, sinks, j):
    S, Dm = x.shape
    n_heads = w_o.shape[1] // ATT_HEAD_DIM
    n_kv = n_heads // ATT_GROUP
    n_qkv = w_qkv.shape[2]
    qkv = proj(hn, [(w_qkv, j, 0)], [], "plain", n_qkv, F32, tn=_wide_tile(n_qkv))
    cos_full, sin_signed = _rope_tables(S)
    o = swa_attention(qkv, sinks[j], cos_full, sin_signed, n_heads, n_kv)
    return _out_proj(o, w_o, j, x, gain)


def _mixer_b(hn, x, gain, w_in, conv_w, a_log, dt_bias, norm_w, w_o, j):
    S, Dm = x.shape
    D = GDN_HEAD_DIM
    conv_w, a_log, dt_bias, norm_w = conv_w[j], a_log[j], dt_bias[j], norm_w[j]
    n_v = a_log.shape[0]
    val_dim = n_v * D
    key_dim = (conv_w.shape[1] - val_dim) // 2
    n_qk = key_dim // D
    n_qkv = 2 * key_dim + val_dim
    tn = _wide_tile(key_dim)
    n_taps = conv_w.shape[0]
    qk = proj(hn, [(w_in, j, 0)], [(conv_w, 0)], "conv_silu_l2", 2 * key_dim, F32, tn=tn,
              n_taps=n_taps, q_tiles=key_dim // tn)
    v = proj(hn, [(w_in, j, 2 * key_dim)], [(conv_w, 2 * key_dim)], "conv_silu", val_dim, F32,
             tn=tn, n_taps=n_taps)
    z = proj(hn, [(w_in, j, n_qkv)], [], "plain", val_dim, F32, tn=tn)
    w_ba = jnp.pad(w_in[j][:, n_qkv + val_dim:], ((0, 0), (0, LANES - 2 * n_v)))
    ba = proj(hn, [(w_ba[None], 0, 0)], [], "plain", LANES, F32, tn=LANES)
    prm = jnp.zeros((2, LANES), F32)
    prm = prm.at[0, n_v:2 * n_v].set(a_log).at[1, n_v:2 * n_v].set(dt_bias)
    o = gated_delta(qk, v, z, ba, prm, norm_w, n_qk, n_v)
    return _out_proj(o, w_o, j, x, gain)


def _mixer_c(hn, x, gain, w_pw1, b_pw1, w_dw, b_dw, ln_g, ln_b, w_pw2, b_pw2, j):
    S, Dm = x.shape
    b1 = b_pw1[j].reshape(1, -1)
    u = proj(hn, [(w_pw1, j, 0), (w_pw1, j, Dm)], [(b1, 0), (b1, Dm)], "glu_bias", Dm, F32, tn=256)
    mid = conformer_mid(u, w_dw[j], b_dw[j], ln_g[j], ln_b[j])
    return _out_proj(mid, w_pw2, j, x, gain, b_pw2[j])


def _mixer_d(hn, x, gain, w_in, w_conv, w_out, j):
    S, Dm = x.shape
    mid = proj(hn, [(w_in, j, 0), (w_in, j, Dm), (w_in, j, 2 * Dm)], [(w_conv[j], 0)],
               "short_conv", Dm, BF16, tn=256, n_taps=w_conv.shape[1])
    return _out_proj(mid, w_out, j, x, gain)


def _ffn(hn, x, gain, w_gate, w_up, w_conv, b_conv, w_down, i):
    d_ff = w_gate.shape[2]
    mid = proj(hn, [(w_gate, i, 0), (w_up, i, 0)], [(w_conv[i], 0), (b_conv[i].reshape(1, -1), 0)],
               "conv_glu", d_ff, BF16, tn=256, n_taps=w_conv.shape[1])
    return down_residual(mid, w_down.astype(BF16), i, x, gain)


def kernel(x, mix_norm, ffn_norm, final_norm, a_w_qkv, a_w_o, a_sinks, b_w_in, b_conv, b_a_log, b_dt_bias, b_norm, b_w_o, c_w_pw1, c_b_pw1, c_w_dw, c_b_dw, c_ln_g, c_ln_b, c_w_pw2, c_b_pw2, d_w_in, d_w_conv, d_w_out, f_w_gate, f_w_up, f_w_conv, f_b_conv, f_w_down):
    B, S, Dm = x.shape
    depth = mix_norm.shape[0]
    outs = []
    for bi in range(B):
        xb = x[bi]
        hn = gain_rstd(xb, mix_norm[0])
        for i in range(depth):
            kind, j = i % 4, i // 4
            g_ffn = ffn_norm[i]
            if kind == 0:
                xb, hn = _mixer_a(hn, xb, g_ffn, a_w_qkv, a_w_o, a_sinks, j)
            elif kind == 1:
                xb, hn = _mixer_b(hn, xb, g_ffn, b_w_in, b_conv, b_a_log, b_dt_bias, b_norm, b_w_o, j)
            elif kind == 2:
                xb, hn = _mixer_c(hn, xb, g_ffn, c_w_pw1, c_b_pw1, c_w_dw, c_b_dw, c_ln_g, c_ln_b,
                                  c_w_pw2, c_b_pw2, j)
            else:
                xb, hn = _mixer_d(hn, xb, g_ffn, d_w_in, d_w_conv, d_w_out, j)
            g_next = mix_norm[i + 1] if i + 1 < depth else final_norm
            xb, hn = _ffn(hn, xb, g_next, f_w_gate, f_w_up, f_w_conv, f_b_conv, f_w_down, i)
        outs.append(rmsnorm(xb, final_norm, F32))
    return jnp.stack(outs, axis=0)
```

```python
import functools
import math

import jax
import jax.numpy as jnp
from jax import lax
from jax.experimental import pallas as pl
from jax.experimental.pallas import tpu as pltpu

F32 = jnp.float32
BF16 = jnp.bfloat16

NORM_EPS = 1e-6
LN_EPS = 1e-5
L2_EPS = 1e-6
ROPE_THETA = 10000.0
ATT_HEAD_DIM = 128
ATT_GROUP = 4
ATT_KV_GROUP = 2
WINDOW = 128
GDN_HEAD_DIM = 128
GDN_CHUNK = 64
GDN_SOLVE_BLOCK = 16
GDN_HEAD_GROUP = 16
CONF_KERNEL = 31

LANES = 128
SUBLANES = 8
VMEM_CAP_BYTES = 61 * 1024 * 1024
VMEM_SLACK_BYTES = 6 * 1024 * 1024


def _cparams(n_axes, vmem_bytes):
    limit = int(min(VMEM_CAP_BYTES, max(vmem_bytes + VMEM_SLACK_BYTES, 16 * 1024 * 1024)))
    return pltpu.CompilerParams(dimension_semantics=("arbitrary",) * n_axes, vmem_limit_bytes=limit)


def _nbytes(shape, dtype):
    return math.prod(shape) * jnp.dtype(dtype).itemsize


def _silu(x):
    return x * jax.nn.sigmoid(x)


def _mm(a, b):
    return jnp.dot(a.astype(BF16), b.astype(BF16), preferred_element_type=F32)


def _mm_nt(a, b):
    return lax.dot_general(a.astype(BF16), b.astype(BF16), (((1,), (1,)), ((), ())),
                           preferred_element_type=F32)


def _rmsnorm_kernel(x_ref, g_ref, o_ref):
    x = x_ref[...]
    ms = jnp.mean(x * x, axis=-1, keepdims=True)
    o_ref[...] = (x * lax.rsqrt(ms + NORM_EPS) * g_ref[...]).astype(o_ref.dtype)


def rmsnorm(x, g, out_dtype, tm=512):
    S, D = x.shape
    tm = min(tm, S)
    vmem = 2 * (_nbytes((tm, D), F32) + _nbytes((tm, D), out_dtype)) + _nbytes((tm, D), F32)
    return pl.pallas_call(
        _rmsnorm_kernel,
        grid=(S // tm,),
        in_specs=[pl.BlockSpec((tm, D), lambda m: (m, 0)),
                  pl.BlockSpec((1, D), lambda m: (0, 0))],
        out_specs=pl.BlockSpec((tm, D), lambda m: (m, 0)),
        out_shape=jax.ShapeDtypeStruct((S, D), out_dtype),
        compiler_params=_cparams(1, vmem),
        name="rmsnorm",
    )(x, g.reshape(1, D))


def _gain_rstd_kernel(x_ref, g_ref, xb_ref, rs_ref):
    x = x_ref[...]
    ms = jnp.mean(x * x, axis=-1, keepdims=True)
    xb_ref[...] = (x * g_ref[...]).astype(xb_ref.dtype)
    rs_ref[...] = jnp.broadcast_to(lax.rsqrt(ms + NORM_EPS), rs_ref.shape)


def gain_rstd(x, g, tm=512):
    S, D = x.shape
    tm = min(tm, S)
    vmem = 2 * (_nbytes((tm, D), F32) + _nbytes((tm, D), BF16)) + _nbytes((tm, D), F32)
    return pl.pallas_call(
        _gain_rstd_kernel,
        grid=(S // tm,),
        in_specs=[pl.BlockSpec((tm, D), lambda m: (m, 0)),
                  pl.BlockSpec((1, D), lambda m: (0, 0))],
        out_specs=[pl.BlockSpec((tm, D), lambda m: (m, 0)),
                   pl.BlockSpec((tm, LANES), lambda m: (m, 0))],
        out_shape=[jax.ShapeDtypeStruct((S, D), BF16), jax.ShapeDtypeStruct((S, LANES), F32)],
        compiler_params=_cparams(1, vmem),
        name="gain_rstd",
    )(x, g.reshape(1, D))


def _causal_taps(ext_ref, w_ref, n_taps, r, rows):
    y = None
    for k in range(n_taps):
        off = SUBLANES - (n_taps - 1 - k) + r
        term = ext_ref[off:off + rows, :] * w_ref[k:k + 1, :]
        y = term if y is None else y + term
    return y


class _TiledParam:
    def __init__(self, refs):
        self.refs = refs

    def __getitem__(self, idx):
        parts = [r[idx] for r in self.refs]
        return parts[0] if len(parts) == 1 else jnp.concatenate(parts, axis=-1)


def _proj_kernel(*refs, n_w, n_p, tiles, mode, tm, tn, n_taps, row_block, row_chunk, q_tiles):
    h_ref, rs_ref = refs[0], refs[1]
    w_refs = refs[2:2 + n_w * tiles]
    p_all = refs[2 + n_w * tiles:2 + (n_w + n_p) * tiles]
    p_refs = [_TiledParam(p_all[k * tiles:(k + 1) * tiles]) for k in range(n_p)]
    o_ref = refs[2 + (n_w + n_p) * tiles]
    wbf_ref = refs[3 + (n_w + n_p) * tiles]
    ext_ref = refs[4 + (n_w + n_p) * tiles] if n_taps else None
    tb = tn // tiles
    n = pl.program_id(0)
    m = pl.program_id(1)

    @pl.when(m == 0)
    def _():
        for i in range(n_w * tiles):
            wbf_ref[:, i * tb:(i + 1) * tb] = w_refs[i][...].astype(BF16)
        if n_taps:
            ext_ref[0:SUBLANES, :] = jnp.zeros((SUBLANES, tn), F32)

    def dots(r0):
        acc = jnp.dot(h_ref[r0:r0 + row_block, :], wbf_ref[...], preferred_element_type=F32)
        rs = rs_ref[r0:r0 + row_block, :]
        rs = jnp.concatenate([rs] * (tn // LANES), axis=1)
        return [acc[:, i * tn:(i + 1) * tn] * rs for i in range(n_w)]

    def to_ext(r0, val):
        ext_ref[SUBLANES + r0:SUBLANES + r0 + row_block, :] = val

    def chunks(r0):
        return [(r, r - r0) for r in range(r0, r0 + row_block, row_chunk)]

    if mode == "plain":
        def matmul(r0):
            return dots(r0)[0]

        def finish(r0, acc):
            o_ref[r0:r0 + row_block, :] = acc.astype(o_ref.dtype)
    elif mode == "glu_bias":
        def matmul(r0):
            return dots(r0)

        def finish(r0, accs):
            val = accs[0] + p_refs[0][...]
            gate = accs[1] + p_refs[1][...]
            o_ref[r0:r0 + row_block, :] = (val * jax.nn.sigmoid(gate)).astype(o_ref.dtype)
    elif mode == "conv_silu":
        def matmul(r0):
            to_ext(r0, dots(r0)[0])

        def finish(r0, _):
            for r, _ in chunks(r0):
                y = _causal_taps(ext_ref, p_refs[0], n_taps, r, row_chunk)
                o_ref[r:r + row_chunk, :] = _silu(y).astype(o_ref.dtype)
    elif mode == "conv_silu_l2":
        scale = jnp.where(n < q_tiles, GDN_HEAD_DIM ** -0.5, 1.0).astype(F32)

        def matmul(r0):
            to_ext(r0, dots(r0)[0])

        def finish(r0, _):
            for r, _ in chunks(r0):
                y = _silu(_causal_taps(ext_ref, p_refs[0], n_taps, r, row_chunk))
                for j in range(0, tn, GDN_HEAD_DIM):
                    yh = y[:, j:j + GDN_HEAD_DIM]
                    inv = lax.rsqrt(jnp.sum(yh * yh, axis=-1, keepdims=True) + L2_EPS)
                    o_ref[r:r + row_chunk, j:j + GDN_HEAD_DIM] = (yh * inv * scale).astype(o_ref.dtype)
    elif mode == "short_conv":
        def matmul(r0):
            bg, cg, xin = dots(r0)
            to_ext(r0, cg * xin)
            return bg

        def finish(r0, bg):
            for r, rr in chunks(r0):
                y = _causal_taps(ext_ref, p_refs[0], n_taps, r, row_chunk)
                o_ref[r:r + row_chunk, :] = (bg[rr:rr + row_chunk, :] * y).astype(o_ref.dtype)
    elif mode == "conv_glu":
        def matmul(r0):
            gate, up = dots(r0)
            to_ext(r0, gate)
            return up

        def finish(r0, up):
            bias = p_refs[1][...]
            for r, rr in chunks(r0):
                y = _causal_taps(ext_ref, p_refs[0], n_taps, r, row_chunk)
                o_ref[r:r + row_chunk, :] = (_silu(y + bias) * up[rr:rr + row_chunk, :]).astype(o_ref.dtype)
    else:
        raise ValueError(mode)

    pending = None
    for r0 in range(0, tm, row_block):
        acc = matmul(r0)
        if pending is not None:
            finish(*pending)
        pending = (r0, acc)
    finish(*pending)
    if n_taps:
        ext_ref[0:SUBLANES, :] = ext_ref[tm:tm + SUBLANES, :]


def proj(hn, weights, params, mode, n_out, out_dtype, *, tn, tiles=1, tm=512, n_taps=0, q_tiles=0,
         row_block=512, row_chunk=128):
    h, rstd = hn
    S, K = h.shape
    tm = min(tm, S)
    row_block = min(row_block, tm)
    assert S % tm == 0 and n_out % tn == 0 and tm % row_block == 0 and row_block % row_chunk == 0
    n_w = len(weights)
    n_blocks = n_out // tn
    in_specs = [pl.BlockSpec((tm, K), lambda n, m: (m, 0)),
                pl.BlockSpec((tm, LANES), lambda n, m: (m, 0))]
    args = [h, rstd]
    vmem = 2 * (_nbytes((tm, K), h.dtype) + _nbytes((tm, LANES), F32))

    def col_block(n, t, o):
        return jnp.minimum(n * tiles + t, n_blocks - 1) + o

    for w, layer, off in weights:
        assert w.shape[1] == K and off % tn == 0
        for t in range(tiles):
            in_specs.append(pl.BlockSpec(
                (None, K, tn), lambda n, m, l=layer, t=t, o=off // tn: (l, 0, col_block(n, t, o))))
            args.append(w)
        vmem += tiles * (2 * _nbytes((K, tn), w.dtype) + _nbytes((K, tn), BF16))
    for p, off in params:
        for t in range(tiles):
            in_specs.append(pl.BlockSpec(
                (p.shape[0], tn), lambda n, m, t=t, o=off // tn: (0, col_block(n, t, o))))
            args.append(p)
        vmem += tiles * 2 * _nbytes((SUBLANES, tn), F32)
    tw = tiles * tn
    scratch = [pltpu.VMEM((K, n_w * tw), BF16)]
    if n_taps:
        scratch.append(pltpu.VMEM((tm + SUBLANES, tw), F32))
        vmem += _nbytes((tm + SUBLANES, tw), F32)
    vmem += 2 * _nbytes((tm, tw), out_dtype) + n_w * _nbytes((row_block, tw), F32)
    kern = functools.partial(_proj_kernel, n_w=n_w, n_p=len(params), tiles=tiles, mode=mode, tm=tm,
                             tn=tw, n_taps=n_taps, row_block=row_block, row_chunk=row_chunk,
                             q_tiles=q_tiles)
    return pl.pallas_call(
        kern,
        grid=(pl.cdiv(n_blocks, tiles), S // tm),
        in_specs=in_specs,
        out_specs=pl.BlockSpec((tm, tw), lambda n, m: (m, n)),
        out_shape=jax.ShapeDtypeStruct((S, n_out), out_dtype),
        scratch_shapes=scratch,
        compiler_params=_cparams(2, vmem),
        name="proj_" + mode,
    )(*args)


def _down_kernel(*refs, has_bias, n_total):
    a_ref, w_ref, x_ref, g_ref = refs[:4]
    b_ref = refs[4] if has_bias else None
    o_ref, xb_ref, rs_ref = refs[-3:]
    n = pl.program_id(1)
    acc = jnp.dot(a_ref[...], w_ref[...], preferred_element_type=F32)
    if has_bias:
        acc = acc + b_ref[...]
    xn = x_ref[...] + acc
    o_ref[...] = xn
    xb_ref[...] = (xn * g_ref[...]).astype(xb_ref.dtype)
    sq = xn * xn
    part = sq[:, 0:LANES]
    for j in range(LANES, sq.shape[1], LANES):
        part = part + sq[:, j:j + LANES]

    @pl.when(n == 0)
    def _():
        rs_ref[...] = part

    @pl.when(n > 0)
    def _():
        rs_ref[...] = rs_ref[...] + part

    @pl.when(n == pl.num_programs(1) - 1)
    def _():
        ms = jnp.sum(rs_ref[...], axis=-1, keepdims=True) / n_total
        rs_ref[...] = jnp.broadcast_to(lax.rsqrt(ms + NORM_EPS), rs_ref.shape)


def down_residual(a, w_bf16, layer, x, gain, bias=None, *, tm=512, tn=512):
    S, K = a.shape
    N = w_bf16.shape[2]
    tm = min(tm, S)
    vmem = 2 * (_nbytes((tm, K), a.dtype) + _nbytes((K, tn), BF16) + 2 * _nbytes((tm, tn), F32)
                + _nbytes((tm, tn), BF16) + _nbytes((tm, LANES), F32))
    vmem += 2 * _nbytes((tm, tn), F32)
    in_specs = [pl.BlockSpec((tm, K), lambda m, n: (m, 0)),
                pl.BlockSpec((None, K, tn), lambda m, n: (layer, 0, n)),
                pl.BlockSpec((tm, tn), lambda m, n: (m, n)),
                pl.BlockSpec((1, tn), lambda m, n: (0, n))]
    args = [a, w_bf16, x, gain.reshape(1, N)]
    if bias is not None:
        in_specs.append(pl.BlockSpec((1, tn), lambda m, n: (0, n)))
        args.append(bias.reshape(1, N))
    x_new, xb, rstd = pl.pallas_call(
        functools.partial(_down_kernel, has_bias=bias is not None, n_total=N),
        grid=(S // tm, N // tn),
        in_specs=in_specs,
        out_specs=[pl.BlockSpec((tm, tn), lambda m, n: (m, n)),
                   pl.BlockSpec((tm, tn), lambda m, n: (m, n)),
                   pl.BlockSpec((tm, LANES), lambda m, n: (m, 0))],
        out_shape=[jax.ShapeDtypeStruct((S, N), F32), jax.ShapeDtypeStruct((S, N), BF16),
                   jax.ShapeDtypeStruct((S, LANES), F32)],
        compiler_params=_cparams(2, vmem),
        name="down_residual",
    )(*args)
    return x_new, (xb, rstd)


def _swa_kernel(sink_ref, q_ref, k_ref, v_ref, cos_ref, sin_ref, o_ref, kprev_ref, vprev_ref,
                *, n_kv, kv_group):
    b = pl.program_id(0)
    D = ATT_HEAD_DIM
    blk = q_ref.shape[0]

    @pl.when(b == 0)
    def _():
        kprev_ref[...] = jnp.zeros(kprev_ref.shape, F32)
        vprev_ref[...] = jnp.zeros(vprev_ref.shape, F32)

    cos = cos_ref[...]
    sin = sin_ref[...]

    def rope(x):
        return x * cos + pltpu.roll(x, D // 2, 1) * sin

    qi = lax.broadcasted_iota(jnp.int32, (blk, 2 * blk), 0)
    kj = lax.broadcasted_iota(jnp.int32, (blk, 2 * blk), 1)
    rel = blk + qi - kj
    valid = (rel >= 0) & (rel < WINDOW) & ((kj >= blk) | (b > 0))
    scale = D ** -0.5

    for g0 in range(0, n_kv, kv_group):
        gs = range(g0, g0 + kv_group)
        kband, vband, s, p = {}, {}, {}, {}
        for g in gs:
            kc = rope(k_ref[:, g * D:(g + 1) * D])
            vc = v_ref[:, g * D:(g + 1) * D]
            kband[g] = jnp.concatenate([kprev_ref[:, g * D:(g + 1) * D], kc], axis=0).astype(BF16)
            vband[g] = jnp.concatenate([vprev_ref[:, g * D:(g + 1) * D], vc], axis=0).astype(BF16)
            kprev_ref[:, g * D:(g + 1) * D] = kc
            vprev_ref[:, g * D:(g + 1) * D] = vc
        hs = [(g, g * ATT_GROUP + gg) for g in gs for gg in range(ATT_GROUP)]
        for g, hh in hs:
            qh = rope(q_ref[:, hh * D:(hh + 1) * D])
            s[hh] = _mm_nt(qh, kband[g]) * scale
        for g, hh in hs:
            sh = jnp.where(valid, s[hh], -jnp.inf)
            sink = sink_ref[hh]
            mx = jnp.maximum(jnp.max(sh, axis=-1, keepdims=True), sink)
            e = jnp.exp(sh - mx)
            denom = jnp.sum(e, axis=-1, keepdims=True) + jnp.exp(sink - mx)
            p[hh] = e / denom
        for g, hh in hs:
            o_ref[:, hh * D:(hh + 1) * D] = _mm(p[hh], vband[g]).astype(o_ref.dtype)


def swa_attention(qkv, sinks, cos_full, sin_signed, n_heads, n_kv):
    S = qkv.shape[0]
    D = ATT_HEAD_DIM
    blk = WINDOW
    qw, kw = n_heads * D, n_kv * D
    vmem = 2 * (_nbytes((blk, qw), F32) + 2 * _nbytes((blk, kw), F32) + 2 * _nbytes((blk, D), F32)
                + _nbytes((blk, qw), BF16)) + 2 * _nbytes((blk, kw), F32)
    return pl.pallas_call(
        functools.partial(_swa_kernel, n_kv=n_kv, kv_group=math.gcd(n_kv, ATT_KV_GROUP)),
        grid=(S // blk,),
        in_specs=[pl.BlockSpec(memory_space=pltpu.SMEM),
                  pl.BlockSpec((blk, qw), lambda b: (b, 0)),
                  pl.BlockSpec((blk, kw), lambda b: (b, qw // kw)),
                  pl.BlockSpec((blk, kw), lambda b: (b, qw // kw + 1)),
                  pl.BlockSpec((blk, D), lambda b: (b, 0)),
                  pl.BlockSpec((blk, D), lambda b: (b, 0))],
        out_specs=pl.BlockSpec((blk, qw), lambda b: (b, 0)),
        out_shape=jax.ShapeDtypeStruct((S, qw), BF16),
        scratch_shapes=[pltpu.VMEM((blk, kw), F32), pltpu.VMEM((blk, kw), F32)],
        compiler_params=_cparams(1, vmem),
        name="swa_attention",
    )(sinks, qkv, qkv, qkv, cos_full, sin_signed)


def _delta_kernel(q_ref, k_ref, v_ref, z_ref, ba_ref, prm_ref, nw_ref, o_ref, st_ref,
                  *, n_v, rep, head_group):
    c = pl.program_id(0)
    C = q_ref.shape[0]
    D = GDN_HEAD_DIM

    @pl.when(c == 0)
    def _():
        st_ref[...] = jnp.zeros(st_ref.shape, F32)

    ba = ba_ref[...]
    beta_t = jax.nn.sigmoid(ba)
    a_sh = ba + prm_ref[1:2, :]
    softplus = jnp.maximum(a_sh, 0.0) + jnp.log(1.0 + jnp.exp(-jnp.abs(a_sh)))
    g_t = -jnp.exp(prm_ref[0:1, :]) * softplus
    row = lax.broadcasted_iota(jnp.int32, (C, LANES), 0)
    gc = g_t
    s = 1
    while s < C:
        gc = gc + jnp.where(row >= s, pltpu.roll(gc, s, 0), 0.0)
        s *= 2
    glast = gc[C - 1:C, :]
    e_gc = jnp.exp(gc)
    e_rest = jnp.exp(glast - gc)
    e_last = jnp.exp(glast)
    gc_t = gc.T

    ii = lax.broadcasted_iota(jnp.int32, (C, C), 0)
    jj = lax.broadcasted_iota(jnp.int32, (C, C), 1)
    lower = ii >= jj
    strict = ii > jj
    same_blk = (ii // GDN_SOLVE_BLOCK) == (jj // GDN_SOLVE_BLOCK)
    nw = nw_ref[...]

    for g0 in range(0, n_v, head_group):
        hs = list(range(g0, g0 + head_group))
        js = sorted({i // rep for i in hs})
        kh = {j: k_ref[:, j * D:(j + 1) * D] for j in js}
        qh = {j: q_ref[:, j * D:(j + 1) * D] for j in js}
        kk = {j: _mm_nt(kh[j], kh[j]) for j in js}
        qk = {j: _mm_nt(qh[j], kh[j]) for j in js}
        st = {i: st_ref[i] for i in hs}
        og, attn, a_o, tp, pw, rhs = {}, {}, {}, {}, {}, {}
        for i in hs:
            j, gi = i // rep, n_v + i
            bcol = beta_t[:, i:i + 1]
            dec = jnp.exp(jnp.where(lower, gc[:, gi:gi + 1] - gc_t[gi:gi + 1, :], -jnp.inf))
            a_kk = jnp.where(strict, bcol * kk[j] * dec, 0.0)
            attn[i] = qk[j] * dec
            og[i] = _mm(qh[j] * e_gc[:, gi:gi + 1], st[i])
            a_d = jnp.where(same_blk, a_kk, 0.0)
            a_o[i] = a_kk - a_d
            tp[i] = -a_d
            pw[i] = tp[i]
            kb = kh[j] * bcol
            rhs[i] = jnp.concatenate([v_ref[:, i * D:(i + 1) * D] * bcol, kb * e_gc[:, gi:gi + 1]],
                                     axis=1)
        n_sq = int(math.log2(GDN_SOLVE_BLOCK)) - 1
        for i in hs:
            pw[i] = _mm(pw[i], pw[i])
        for s in range(n_sq):
            for i in hs:
                tp[i] = tp[i] + pw[i] + _mm(tp[i], pw[i])
            if s + 1 < n_sq:
                for i in hs:
                    pw[i] = _mm(pw[i], pw[i])
        nn = {i: a_o[i] + _mm(tp[i], a_o[i]) for i in hs}
        y = {i: rhs[i] + _mm(tp[i], rhs[i]) for i in hs}
        n2 = {i: _mm(nn[i], nn[i]) for i in hs}
        zz = {i: y[i] + _mm(n2[i], y[i]) for i in hs}
        x = {i: zz[i] - _mm(nn[i], zz[i]) for i in hs}
        v_new = {i: x[i][:, :D] - _mm(x[i][:, D:], st[i]) for i in hs}
        for i in hs:
            j, gi = i // rep, n_v + i
            out = og[i] + _mm(attn[i], v_new[i])
            kd = kh[j] * e_rest[:, gi:gi + 1]
            st_ref[i] = st[i] * e_last[:, gi:gi + 1] + _mm(kd.T, v_new[i])
            ms = jnp.mean(out * out, axis=-1, keepdims=True)
            zh = z_ref[:, i * D:(i + 1) * D]
            o_ref[:, i * D:(i + 1) * D] = (out * lax.rsqrt(ms + NORM_EPS) * nw * _silu(zh)).astype(o_ref.dtype)


def gated_delta(qk, v, z, ba, prm, norm_w, n_qk, n_v):
    S = qk.shape[0]
    D = GDN_HEAD_DIM
    C = GDN_CHUNK
    qw, vw = n_qk * D, n_v * D
    assert GDN_SOLVE_BLOCK * 4 == C
    vmem = 2 * (2 * _nbytes((C, qw), F32) + 2 * _nbytes((C, vw), F32) + _nbytes((C, vw), BF16))
    vmem += _nbytes((n_v, D, D), F32)
    return pl.pallas_call(
        functools.partial(_delta_kernel, n_v=n_v, rep=n_v // n_qk,
                          head_group=math.gcd(n_v, GDN_HEAD_GROUP)),
        grid=(S // C,),
        in_specs=[pl.BlockSpec((C, qw), lambda c: (c, 0)),
                  pl.BlockSpec((C, qw), lambda c: (c, 1)),
                  pl.BlockSpec((C, vw), lambda c: (c, 0)),
                  pl.BlockSpec((C, vw), lambda c: (c, 0)),
                  pl.BlockSpec((C, LANES), lambda c: (c, 0)),
                  pl.BlockSpec((2, LANES), lambda c: (0, 0)),
                  pl.BlockSpec((1, D), lambda c: (0, 0))],
        out_specs=pl.BlockSpec((C, vw), lambda c: (c, 0)),
        out_shape=jax.ShapeDtypeStruct((S, vw), BF16),
        scratch_shapes=[pltpu.VMEM((n_v, D, D), F32)],
        compiler_params=_cparams(1, vmem),
        name="gated_delta",
    )(qk, qk, v, z, ba, prm, norm_w.reshape(1, D))


CONF_HIST = 32


def _conf_kernel(u_ref, w_ref, bdw_ref, g_ref, b_ref, o_ref, ext_ref, sh_ref, y_ref, *, tm, cw,
                 n_c, row_chunk):
    m = pl.program_id(0)
    c = pl.program_id(1)

    @pl.when(m == 0)
    def _():
        ext_ref[c, 0:CONF_HIST, :] = jnp.zeros((CONF_HIST, cw), F32)

    ext_ref[c, CONF_HIST:CONF_HIST + tm, :] = u_ref[...]
    n_rows = CONF_HIST + tm - SUBLANES
    for j in range(1, SUBLANES):
        sh_ref[j - 1, SUBLANES:SUBLANES + n_rows, :] = ext_ref[c, SUBLANES - j:SUBLANES - j + n_rows, :]
    bias = bdw_ref[...]
    for r in range(0, tm, row_chunk):
        y = None
        for k in range(CONF_KERNEL):
            shift = CONF_KERNEL - 1 - k
            a, j = divmod(shift, SUBLANES)
            off = CONF_HIST - SUBLANES * a + r
            if j == 0:
                src = ext_ref[c, off:off + row_chunk, :]
            else:
                src = sh_ref[j - 1, off:off + row_chunk, :]
            term = src * w_ref[k:k + 1, :]
            y = term if y is None else y + term
        y_ref[c, r:r + row_chunk, :] = y + bias
    ext_ref[c, 0:CONF_HIST, :] = ext_ref[c, tm:tm + CONF_HIST, :]

    @pl.when(c == n_c - 1)
    def _():
        d = n_c * cw
        tot = jnp.zeros((tm, 1), F32)
        for j in range(n_c):
            tot = tot + jnp.sum(y_ref[j], axis=-1, keepdims=True)
        mean = tot / d
        var = jnp.zeros((tm, 1), F32)
        for j in range(n_c):
            yc = y_ref[j] - mean
            var = var + jnp.sum(yc * yc, axis=-1, keepdims=True)
        inv = lax.rsqrt(var / d + LN_EPS)
        for j in range(n_c):
            zz = (y_ref[j] - mean) * inv * g_ref[:, j * cw:(j + 1) * cw] + b_ref[:, j * cw:(j + 1) * cw]
            o_ref[:, j * cw:(j + 1) * cw] = _silu(zz).astype(o_ref.dtype)


def conformer_mid(u, w_dw, b_dw, ln_g, ln_b, *, tm=256, cw=512, row_chunk=32):
    S, D = u.shape
    tm, cw = min(tm, S), min(cw, D)
    n_c = D // cw
    vmem = 2 * (_nbytes((tm, cw), F32) + _nbytes((CONF_HIST, cw), F32) + _nbytes((tm, D), BF16))
    vmem += _nbytes((n_c + SUBLANES - 1, tm + CONF_HIST, cw), F32) + 3 * _nbytes((n_c, tm, cw), F32)
    kern = functools.partial(_conf_kernel, tm=tm, cw=cw, n_c=n_c, row_chunk=row_chunk)
    return pl.pallas_call(
        kern,
        grid=(S // tm, n_c),
        in_specs=[pl.BlockSpec((tm, cw), lambda m, c: (m, c)),
                  pl.BlockSpec((CONF_KERNEL, cw), lambda m, c: (0, c)),
                  pl.BlockSpec((1, cw), lambda m, c: (0, c)),
                  pl.BlockSpec((1, D), lambda m, c: (0, 0)),
                  pl.BlockSpec((1, D), lambda m, c: (0, 0))],
        out_specs=pl.BlockSpec((tm, D), lambda m, c: (m, 0)),
        out_shape=jax.ShapeDtypeStruct((S, D), BF16),
        scratch_shapes=[pltpu.VMEM((n_c, tm + CONF_HIST, cw), F32),
                        pltpu.VMEM((SUBLANES - 1, tm + CONF_HIST, cw), F32),
                        pltpu.VMEM((n_c, tm, cw), F32)],
        compiler_params=_cparams(2, vmem),
        name="conformer_mid",
    )(u, w_dw, b_dw.reshape(1, D), ln_g.reshape(1, D), ln_b.reshape(1, D))


def _rope_tables(S):
    half = ATT_HEAD_DIM // 2
    inv = jnp.power(ROPE_THETA, -jnp.arange(half, dtype=F32) / half)
    ang = jnp.arange(S).astype(F32)[:, None] * inv[None, :]
    cos, sin = jnp.cos(ang), jnp.sin(ang)
    return jnp.concatenate([cos, cos], axis=1), jnp.concatenate([-sin, sin], axis=1)


def _wide_tile(n, cap=512):
    t = cap
    while n % t:
        t //= 2
    assert t >= LANES
    return t


PROJ_COLS = 1024


def _out_proj(a, w_o, j, x, gain, bias=None):
    return down_residual(a, w_o.astype(BF16), j, x, gain, bias, tm=1024, tn=_wide_tile(x.shape[1]))


def _mixer_a(hn, x, gain, w_qkv, w_o, sinks, j):
    S, Dm = x.shape
    n_heads = w_o.shape[1] // ATT_HEAD_DIM
    n_kv = n_heads // ATT_GROUP
    n_qkv = w_qkv.shape[2]
    qkv = proj(hn, [(w_qkv, j, 0)], [], "plain", n_qkv, F32, tn=_wide_tile(n_qkv, PROJ_COLS))
    cos_full, sin_signed = _rope_tables(S)
    o = swa_attention(qkv, sinks[j], cos_full, sin_signed, n_heads, n_kv)
    return _out_proj(o, w_o, j, x, gain)


def _mixer_b(hn, x, gain, w_in, conv_w, a_log, dt_bias, norm_w, w_o, j):
    S, Dm = x.shape
    D = GDN_HEAD_DIM
    conv_w, a_log, dt_bias, norm_w = conv_w[j], a_log[j], dt_bias[j], norm_w[j]
    n_v = a_log.shape[0]
    val_dim = n_v * D
    key_dim = (conv_w.shape[1] - val_dim) // 2
    n_qk = key_dim // D
    n_qkv = 2 * key_dim + val_dim
    tn = _wide_tile(key_dim, PROJ_COLS)
    n_taps = conv_w.shape[0]
    qk = proj(hn, [(w_in, j, 0)], [(conv_w, 0)], "conv_silu_l2", 2 * key_dim, F32, tn=tn,
              n_taps=n_taps, q_tiles=key_dim // tn)
    v = proj(hn, [(w_in, j, 2 * key_dim)], [(conv_w, 2 * key_dim)], "conv_silu", val_dim, F32,
             tn=tn, n_taps=n_taps)
    z = proj(hn, [(w_in, j, n_qkv)], [], "plain", val_dim, F32, tn=tn)
    w_ba = jnp.pad(w_in[j][:, n_qkv + val_dim:], ((0, 0), (0, LANES - 2 * n_v)))
    ba = proj(hn, [(w_ba[None], 0, 0)], [], "plain", LANES, F32, tn=LANES)
    prm = jnp.zeros((2, LANES), F32)
    prm = prm.at[0, n_v:2 * n_v].set(a_log).at[1, n_v:2 * n_v].set(dt_bias)
    o = gated_delta(qk, v, z, ba, prm, norm_w, n_qk, n_v)
    return _out_proj(o, w_o, j, x, gain)


def _mixer_c(hn, x, gain, w_pw1, b_pw1, w_dw, b_dw, ln_g, ln_b, w_pw2, b_pw2, j):
    S, Dm = x.shape
    b1 = b_pw1[j].reshape(1, -1)
    u = proj(hn, [(w_pw1, j, 0), (w_pw1, j, Dm)], [(b1, 0), (b1, Dm)], "glu_bias", Dm, F32,
             tn=_wide_tile(Dm, PROJ_COLS // 2))
    mid = conformer_mid(u, w_dw[j], b_dw[j], ln_g[j], ln_b[j])
    return _out_proj(mid, w_pw2, j, x, gain, b_pw2[j])


def _mixer_d(hn, x, gain, w_in, w_conv, w_out, j):
    S, Dm = x.shape
    mid = proj(hn, [(w_in, j, 0), (w_in, j, Dm), (w_in, j, 2 * Dm)], [(w_conv[j], 0)],
               "short_conv", Dm, BF16, tn=256, tm=1024, row_block=1024, n_taps=w_conv.shape[1])
    return _out_proj(mid, w_out, j, x, gain)


def _ffn(hn, x, gain, w_gate, w_up, w_conv, b_conv, w_down, i):
    d_ff = w_gate.shape[2]
    mid = proj(hn, [(w_gate, i, 0), (w_up, i, 0)], [(w_conv[i], 0), (b_conv[i].reshape(1, -1), 0)],
               "conv_glu", d_ff, BF16, tn=256, tiles=PROJ_COLS // 512, n_taps=w_conv.shape[1])
    return down_residual(mid, w_down.astype(BF16), i, x, gain)


def kernel(x, mix_norm, ffn_norm, final_norm, a_w_qkv, a_w_o, a_sinks, b_w_in, b_conv, b_a_log, b_dt_bias, b_norm, b_w_o, c_w_pw1, c_b_pw1, c_w_dw, c_b_dw, c_ln_g, c_ln_b, c_w_pw2, c_b_pw2, d_w_in, d_w_conv, d_w_out, f_w_gate, f_w_up, f_w_conv, f_b_conv, f_w_down):
    B, S, Dm = x.shape
    depth = mix_norm.shape[0]
    outs = []
    for bi in range(B):
        xb = x[bi]
        hn = gain_rstd(xb, mix_norm[0])
        for i in range(depth):
            kind, j = i % 4, i // 4
            g_ffn = ffn_norm[i]
            if kind == 0:
                xb, hn = _mixer_a(hn, xb, g_ffn, a_w_qkv, a_w_o, a_sinks, j)
            elif kind == 1:
                xb, hn = _mixer_b(hn, xb, g_ffn, b_w_in, b_conv, b_a_log, b_dt_bias, b_norm, b_w_o, j)
            elif kind == 2:
                xb, hn = _mixer_c(hn, xb, g_ffn, c_w_pw1, c_b_pw1, c_w_dw, c_b_dw, c_ln_g, c_ln_b,
                                  c_w_pw2, c_b_pw2, j)
            else:
                xb, hn = _mixer_d(hn, xb, g_ffn, d_w_in, d_w_conv, d_w_out, j)
            g_next = mix_norm[i + 1] if i + 1 < depth else final_norm
            xb, hn = _ffn(hn, xb, g_next, f_w_gate, f_w_up, f_w_conv, f_b_conv, f_w_down, i)
        outs.append(rmsnorm(xb, final_norm, F32))
    return jnp.stack(outs, axis=0)
```

```python
import functools
import math

import jax
import jax.numpy as jnp
from jax import lax
from jax.experimental import pallas as pl
from jax.experimental.pallas import tpu as pltpu

F32 = jnp.float32
BF16 = jnp.bfloat16

NORM_EPS = 1e-6
LN_EPS = 1e-5
L2_EPS = 1e-6
ROPE_THETA = 10000.0
ATT_HEAD_DIM = 128
ATT_GROUP = 4
ATT_KV_GROUP = 2
WINDOW = 128
GDN_HEAD_DIM = 128
GDN_CHUNK = 64
GDN_SOLVE_BLOCK = 16
GDN_HEAD_GROUP = 16
CONF_KERNEL = 31

LANES = 128
SUBLANES = 8
VMEM_CAP_BYTES = 61 * 1024 * 1024
VMEM_SLACK_BYTES = 14 * 1024 * 1024


def _cparams(n_axes, vmem_bytes):
    limit = int(min(VMEM_CAP_BYTES, max(vmem_bytes + VMEM_SLACK_BYTES, 16 * 1024 * 1024)))
    return pltpu.CompilerParams(dimension_semantics=("arbitrary",) * n_axes, vmem_limit_bytes=limit)


def _nbytes(shape, dtype):
    return math.prod(shape) * jnp.dtype(dtype).itemsize


def _silu(x):
    return x * jax.nn.sigmoid(x)


def _mm(a, b):
    return jnp.dot(a.astype(BF16), b.astype(BF16), preferred_element_type=F32)


def _mm_nt(a, b):
    return lax.dot_general(a.astype(BF16), b.astype(BF16), (((1,), (1,)), ((), ())),
                           preferred_element_type=F32)


def _rmsnorm_kernel(x_ref, g_ref, o_ref):
    x = x_ref[...]
    ms = jnp.mean(x * x, axis=-1, keepdims=True)
    o_ref[...] = (x * lax.rsqrt(ms + NORM_EPS) * g_ref[...]).astype(o_ref.dtype)


def rmsnorm(x, g, out_dtype, tm=512):
    S, D = x.shape
    tm = min(tm, S)
    vmem = 2 * (_nbytes((tm, D), F32) + _nbytes((tm, D), out_dtype)) + _nbytes((tm, D), F32)
    return pl.pallas_call(
        _rmsnorm_kernel,
        grid=(S // tm,),
        in_specs=[pl.BlockSpec((tm, D), lambda m: (m, 0)),
                  pl.BlockSpec((1, D), lambda m: (0, 0))],
        out_specs=pl.BlockSpec((tm, D), lambda m: (m, 0)),
        out_shape=jax.ShapeDtypeStruct((S, D), out_dtype),
        compiler_params=_cparams(1, vmem),
        name="rmsnorm",
    )(x, g.reshape(1, D))


def _gain_rstd_kernel(x_ref, g_ref, xb_ref, rs_ref):
    x = x_ref[...]
    ms = jnp.mean(x * x, axis=-1, keepdims=True)
    xb_ref[...] = (x * g_ref[...]).astype(xb_ref.dtype)
    rs_ref[...] = jnp.broadcast_to(lax.rsqrt(ms + NORM_EPS), rs_ref.shape)


def gain_rstd(x, g, tm=512):
    S, D = x.shape
    tm = min(tm, S)
    vmem = 2 * (_nbytes((tm, D), F32) + _nbytes((tm, D), BF16)) + _nbytes((tm, D), F32)
    return pl.pallas_call(
        _gain_rstd_kernel,
        grid=(S // tm,),
        in_specs=[pl.BlockSpec((tm, D), lambda m: (m, 0)),
                  pl.BlockSpec((1, D), lambda m: (0, 0))],
        out_specs=[pl.BlockSpec((tm, D), lambda m: (m, 0)),
                   pl.BlockSpec((tm, LANES), lambda m: (m, 0))],
        out_shape=[jax.ShapeDtypeStruct((S, D), BF16), jax.ShapeDtypeStruct((S, LANES), F32)],
        compiler_params=_cparams(1, vmem),
        name="gain_rstd",
    )(x, g.reshape(1, D))


def _causal_taps(ext_ref, w_ref, n_taps, r, rows):
    y = None
    for k in range(n_taps):
        off = SUBLANES - (n_taps - 1 - k) + r
        term = ext_ref[off:off + rows, :] * w_ref[k:k + 1, :]
        y = term if y is None else y + term
    return y


class _TiledParam:
    def __init__(self, refs):
        self.refs = refs

    def __getitem__(self, idx):
        parts = [r[idx] for r in self.refs]
        return parts[0] if len(parts) == 1 else jnp.concatenate(parts, axis=-1)


def _proj_kernel(*refs, w_src, n_blocks, n_p, tiles, mode, tm, tn, n_taps, row_block, row_chunk,
                 q_tiles):
    n_w = len(w_src)
    h_ref, rs_ref = refs[0], refs[1]
    w_hbm = refs[2:2 + n_w]
    p_all = refs[2 + n_w:2 + n_w + n_p * tiles]
    p_refs = [_TiledParam(p_all[k * tiles:(k + 1) * tiles]) for k in range(n_p)]
    o_ref = refs[2 + n_w + n_p * tiles]
    stage_ref, sem, wbf_ref = refs[3 + n_w + n_p * tiles:6 + n_w + n_p * tiles]
    ext_ref = refs[6 + n_w + n_p * tiles] if n_taps else None
    tb = tn // tiles
    n = pl.program_id(0)
    m = pl.program_id(1)

    def weight_copies(step):
        copies = []
        for i, (layer, first_block) in enumerate(w_src):
            for t in range(tiles):
                blk = jnp.minimum(step * tiles + t, n_blocks - 1) + first_block
                src = w_hbm[i].at[layer, :, pl.ds(pl.multiple_of(blk * tb, tb), tb)]
                k = i * tiles + t
                copies.append(pltpu.make_async_copy(src, stage_ref.at[k], sem.at[k]))
        return copies

    @pl.when(m == 0)
    def _():
        @pl.when(n == 0)
        def _():
            for c in weight_copies(n):
                c.start()

        for k, c in enumerate(weight_copies(n)):
            c.wait()
            wbf_ref[:, k * tb:(k + 1) * tb] = stage_ref[k].astype(BF16)

        @pl.when(n + 1 < pl.num_programs(0))
        def _():
            for c in weight_copies(n + 1):
                c.start()

        if n_taps:
            ext_ref[0:SUBLANES, :] = jnp.zeros((SUBLANES, tn), F32)

    def dots(r0):
        acc = jnp.dot(h_ref[r0:r0 + row_block, :], wbf_ref[...], preferred_element_type=F32)
        rs = rs_ref[r0:r0 + row_block, :]
        rs = jnp.concatenate([rs] * (tn // LANES), axis=1)
        return [acc[:, i * tn:(i + 1) * tn] * rs for i in range(n_w)]

    def to_ext(r0, val):
        ext_ref[SUBLANES + r0:SUBLANES + r0 + row_block, :] = val

    def chunks(r0):
        return [(r, r - r0) for r in range(r0, r0 + row_block, row_chunk)]

    if mode == "plain":
        def matmul(r0):
            return dots(r0)[0]

        def finish(r0, acc):
            o_ref[r0:r0 + row_block, :] = acc.astype(o_ref.dtype)
    elif mode == "glu_bias":
        def matmul(r0):
            return dots(r0)

        def finish(r0, accs):
            val = accs[0] + p_refs[0][...]
            gate = accs[1] + p_refs[1][...]
            o_ref[r0:r0 + row_block, :] = (val * jax.nn.sigmoid(gate)).astype(o_ref.dtype)
    elif mode == "conv_silu":
        def matmul(r0):
            to_ext(r0, dots(r0)[0])

        def finish(r0, _):
            for r, _ in chunks(r0):
                y = _causal_taps(ext_ref, p_refs[0], n_taps, r, row_chunk)
                o_ref[r:r + row_chunk, :] = _silu(y).astype(o_ref.dtype)
    elif mode == "conv_silu_l2":
        scale = jnp.where(n < q_tiles, GDN_HEAD_DIM ** -0.5, 1.0).astype(F32)

        def matmul(r0):
            to_ext(r0, dots(r0)[0])

        def finish(r0, _):
            for r, _ in chunks(r0):
                y = _silu(_causal_taps(ext_ref, p_refs[0], n_taps, r, row_chunk))
                for j in range(0, tn, GDN_HEAD_DIM):
                    yh = y[:, j:j + GDN_HEAD_DIM]
                    inv = lax.rsqrt(jnp.sum(yh * yh, axis=-1, keepdims=True) + L2_EPS)
                    o_ref[r:r + row_chunk, j:j + GDN_HEAD_DIM] = (yh * inv * scale).astype(o_ref.dtype)
    elif mode == "short_conv":
        def matmul(r0):
            bg, cg, xin = dots(r0)
            to_ext(r0, cg * xin)
            return bg

        def finish(r0, bg):
            for r, rr in chunks(r0):
                y = _causal_taps(ext_ref, p_refs[0], n_taps, r, row_chunk)
                o_ref[r:r + row_chunk, :] = (bg[rr:rr + row_chunk, :] * y).astype(o_ref.dtype)
    elif mode == "conv_glu":
        def matmul(r0):
            gate, up = dots(r0)
            to_ext(r0, gate)
            return up

        def finish(r0, up):
            bias = p_refs[1][...]
            for r, rr in chunks(r0):
                y = _causal_taps(ext_ref, p_refs[0], n_taps, r, row_chunk)
                o_ref[r:r + row_chunk, :] = (_silu(y + bias) * up[rr:rr + row_chunk, :]).astype(o_ref.dtype)
    else:
        raise ValueError(mode)

    pending = None
    for r0 in range(0, tm, row_block):
        acc = matmul(r0)
        if pending is not None:
            finish(*pending)
        pending = (r0, acc)
    finish(*pending)
    if n_taps:
        ext_ref[0:SUBLANES, :] = ext_ref[tm:tm + SUBLANES, :]


def proj(hn, weights, params, mode, n_out, out_dtype, *, tn, tiles=1, tm=None, n_taps=0, q_tiles=0,
         row_block=512, row_chunk=128):
    h, rstd = hn
    S, K = h.shape
    if tm is None:
        tm = 1024 if jnp.dtype(out_dtype).itemsize == 2 else 512
    tm = min(tm, S)
    row_block = min(row_block, tm)
    assert S % tm == 0 and n_out % tn == 0 and tm % row_block == 0 and row_block % row_chunk == 0
    n_w = len(weights)
    n_blocks = n_out // tn
    in_specs = [pl.BlockSpec((tm, K), lambda n, m: (m, 0)),
                pl.BlockSpec((tm, LANES), lambda n, m: (m, 0))]
    args = [h, rstd]
    vmem = 2 * (_nbytes((tm, K), h.dtype) + _nbytes((tm, LANES), F32))

    def col_block(n, t, o):
        return jnp.minimum(n * tiles + t, n_blocks - 1) + o

    w_src = []
    for w, layer, off in weights:
        assert w.shape[1] == K and off % tn == 0 and w.dtype == F32
        in_specs.append(pl.BlockSpec(memory_space=pl.ANY))
        args.append(w)
        w_src.append((layer, off // tn))
        vmem += tiles * (_nbytes((K, tn), F32) + _nbytes((K, tn), BF16))
    for p, off in params:
        for t in range(tiles):
            in_specs.append(pl.BlockSpec(
                (p.shape[0], tn), lambda n, m, t=t, o=off // tn: (0, col_block(n, t, o))))
            args.append(p)
        vmem += tiles * 2 * _nbytes((SUBLANES, tn), F32)
    tw = tiles * tn
    scratch = [pltpu.VMEM((n_w * tiles, K, tn), F32),
               pltpu.SemaphoreType.DMA((n_w * tiles,)),
               pltpu.VMEM((K, n_w * tw), BF16)]
    if n_taps:
        scratch.append(pltpu.VMEM((tm + SUBLANES, tw), F32))
        vmem += _nbytes((tm + SUBLANES, tw), F32)
    vmem += 2 * _nbytes((tm, tw), out_dtype) + n_w * _nbytes((row_block, tw), F32)
    kern = functools.partial(_proj_kernel, w_src=tuple(w_src), n_blocks=n_blocks, n_p=len(params),
                             tiles=tiles, mode=mode, tm=tm, tn=tw, n_taps=n_taps,
                             row_block=row_block, row_chunk=row_chunk, q_tiles=q_tiles)
    return pl.pallas_call(
        kern,
        grid=(pl.cdiv(n_blocks, tiles), S // tm),
        in_specs=in_specs,
        out_specs=pl.BlockSpec((tm, tw), lambda n, m: (m, n)),
        out_shape=jax.ShapeDtypeStruct((S, n_out), out_dtype),
        scratch_shapes=scratch,
        compiler_params=_cparams(2, vmem),
        name="proj_" + mode,
    )(*args)


def _down_kernel(*refs, has_bias, n_total):
    a_ref, w_ref, x_ref, g_ref = refs[:4]
    b_ref = refs[4] if has_bias else None
    o_ref, xb_ref, rs_ref = refs[-3:]
    n = pl.program_id(1)
    acc = jnp.dot(a_ref[...], w_ref[...], preferred_element_type=F32)
    if has_bias:
        acc = acc + b_ref[...]
    xn = x_ref[...] + acc
    o_ref[...] = xn
    xb_ref[...] = (xn * g_ref[...]).astype(xb_ref.dtype)
    sq = xn * xn
    part = sq[:, 0:LANES]
    for j in range(LANES, sq.shape[1], LANES):
        part = part + sq[:, j:j + LANES]

    @pl.when(n == 0)
    def _():
        rs_ref[...] = part

    @pl.when(n > 0)
    def _():
        rs_ref[...] = rs_ref[...] + part

    @pl.when(n == pl.num_programs(1) - 1)
    def _():
        ms = jnp.sum(rs_ref[...], axis=-1, keepdims=True) / n_total
        rs_ref[...] = jnp.broadcast_to(lax.rsqrt(ms + NORM_EPS), rs_ref.shape)


def down_residual(a, w_bf16, layer, x, gain, bias=None, *, tm=512, tn=512):
    S, K = a.shape
    N = w_bf16.shape[2]
    tm = min(tm, S)
    vmem = 2 * (_nbytes((tm, K), a.dtype) + _nbytes((K, tn), BF16) + 2 * _nbytes((tm, tn), F32)
                + _nbytes((tm, tn), BF16) + _nbytes((tm, LANES), F32))
    vmem += 2 * _nbytes((tm, tn), F32)
    in_specs = [pl.BlockSpec((tm, K), lambda m, n: (m, 0)),
                pl.BlockSpec((None, K, tn), lambda m, n: (layer, 0, n)),
                pl.BlockSpec((tm, tn), lambda m, n: (m, n)),
                pl.BlockSpec((1, tn), lambda m, n: (0, n))]
    args = [a, w_bf16, x, gain.reshape(1, N)]
    if bias is not None:
        in_specs.append(pl.BlockSpec((1, tn), lambda m, n: (0, n)))
        args.append(bias.reshape(1, N))
    x_new, xb, rstd = pl.pallas_call(
        functools.partial(_down_kernel, has_bias=bias is not None, n_total=N),
        grid=(S // tm, N // tn),
        in_specs=in_specs,
        out_specs=[pl.BlockSpec((tm, tn), lambda m, n: (m, n)),
                   pl.BlockSpec((tm, tn), lambda m, n: (m, n)),
                   pl.BlockSpec((tm, LANES), lambda m, n: (m, 0))],
        out_shape=[jax.ShapeDtypeStruct((S, N), F32), jax.ShapeDtypeStruct((S, N), BF16),
                   jax.ShapeDtypeStruct((S, LANES), F32)],
        compiler_params=_cparams(2, vmem),
        name="down_residual",
    )(*args)
    return x_new, (xb, rstd)


def _swa_kernel(sink_ref, q_ref, k_ref, v_ref, cos_ref, sin_ref, o_ref, kprev_ref, vprev_ref,
                *, n_kv, kv_group):
    b = pl.program_id(0)
    D = ATT_HEAD_DIM
    blk = q_ref.shape[0]

    @pl.when(b == 0)
    def _():
        kprev_ref[...] = jnp.zeros(kprev_ref.shape, F32)
        vprev_ref[...] = jnp.zeros(vprev_ref.shape, F32)

    cos = cos_ref[...]
    sin = sin_ref[...]

    def rope(x):
        return x * cos + pltpu.roll(x, D // 2, 1) * sin

    qi = lax.broadcasted_iota(jnp.int32, (blk, 2 * blk), 0)
    kj = lax.broadcasted_iota(jnp.int32, (blk, 2 * blk), 1)
    rel = blk + qi - kj
    valid = (rel >= 0) & (rel < WINDOW) & ((kj >= blk) | (b > 0))
    scale = D ** -0.5

    for g0 in range(0, n_kv, kv_group):
        gs = range(g0, g0 + kv_group)
        kband, vband, s, p = {}, {}, {}, {}
        for g in gs:
            kc = rope(k_ref[:, g * D:(g + 1) * D])
            vc = v_ref[:, g * D:(g + 1) * D]
            kband[g] = jnp.concatenate([kprev_ref[:, g * D:(g + 1) * D], kc], axis=0).astype(BF16)
            vband[g] = jnp.concatenate([vprev_ref[:, g * D:(g + 1) * D], vc], axis=0).astype(BF16)
            kprev_ref[:, g * D:(g + 1) * D] = kc
            vprev_ref[:, g * D:(g + 1) * D] = vc
        hs = [(g, g * ATT_GROUP + gg) for g in gs for gg in range(ATT_GROUP)]
        for g, hh in hs:
            qh = rope(q_ref[:, hh * D:(hh + 1) * D])
            s[hh] = _mm_nt(qh, kband[g]) * scale
        for g, hh in hs:
            sh = jnp.where(valid, s[hh], -jnp.inf)
            sink = sink_ref[hh]
            mx = jnp.maximum(jnp.max(sh, axis=-1, keepdims=True), sink)
            e = jnp.exp(sh - mx)
            denom = jnp.sum(e, axis=-1, keepdims=True) + jnp.exp(sink - mx)
            p[hh] = e / denom
        for g, hh in hs:
            o_ref[:, hh * D:(hh + 1) * D] = _mm(p[hh], vband[g]).astype(o_ref.dtype)


def swa_attention(qkv, sinks, cos_full, sin_signed, n_heads, n_kv):
    S = qkv.shape[0]
    D = ATT_HEAD_DIM
    blk = WINDOW
    qw, kw = n_heads * D, n_kv * D
    vmem = 2 * (_nbytes((blk, qw), F32) + 2 * _nbytes((blk, kw), F32) + 2 * _nbytes((blk, D), F32)
                + _nbytes((blk, qw), BF16)) + 2 * _nbytes((blk, kw), F32)
    return pl.pallas_call(
        functools.partial(_swa_kernel, n_kv=n_kv, kv_group=math.gcd(n_kv, ATT_KV_GROUP)),
        grid=(S // blk,),
        in_specs=[pl.BlockSpec(memory_space=pltpu.SMEM),
                  pl.BlockSpec((blk, qw), lambda b: (b, 0)),
                  pl.BlockSpec((blk, kw), lambda b: (b, qw // kw)),
                  pl.BlockSpec((blk, kw), lambda b: (b, qw // kw + 1)),
                  pl.BlockSpec((blk, D), lambda b: (b, 0)),
                  pl.BlockSpec((blk, D), lambda b: (b, 0))],
        out_specs=pl.BlockSpec((blk, qw), lambda b: (b, 0)),
        out_shape=jax.ShapeDtypeStruct((S, qw), BF16),
        scratch_shapes=[pltpu.VMEM((blk, kw), F32), pltpu.VMEM((blk, kw), F32)],
        compiler_params=_cparams(1, vmem),
        name="swa_attention",
    )(sinks, qkv, qkv, qkv, cos_full, sin_signed)


def _delta_kernel(q_ref, k_ref, v_ref, z_ref, ba_ref, prm_ref, nw_ref, o_ref, st_ref,
                  *, n_v, rep, head_group):
    c = pl.program_id(0)
    C = q_ref.shape[0]
    D = GDN_HEAD_DIM

    @pl.when(c == 0)
    def _():
        st_ref[...] = jnp.zeros(st_ref.shape, F32)

    ba = ba_ref[...]
    beta_t = jax.nn.sigmoid(ba)
    a_sh = ba + prm_ref[1:2, :]
    softplus = jnp.maximum(a_sh, 0.0) + jnp.log(1.0 + jnp.exp(-jnp.abs(a_sh)))
    g_t = -jnp.exp(prm_ref[0:1, :]) * softplus
    row = lax.broadcasted_iota(jnp.int32, (C, LANES), 0)
    gc = g_t
    s = 1
    while s < C:
        gc = gc + jnp.where(row >= s, pltpu.roll(gc, s, 0), 0.0)
        s *= 2
    glast = gc[C - 1:C, :]
    e_gc = jnp.exp(gc)
    e_rest = jnp.exp(glast - gc)
    e_last = jnp.exp(glast)
    gc_t = gc.T

    ii = lax.broadcasted_iota(jnp.int32, (C, C), 0)
    jj = lax.broadcasted_iota(jnp.int32, (C, C), 1)
    lower = ii >= jj
    strict = ii > jj
    same_blk = (ii // GDN_SOLVE_BLOCK) == (jj // GDN_SOLVE_BLOCK)
    nw = nw_ref[...]

    for g0 in range(0, n_v, head_group):
        hs = list(range(g0, g0 + head_group))
        js = sorted({i // rep for i in hs})
        kh = {j: k_ref[:, j * D:(j + 1) * D] for j in js}
        qh = {j: q_ref[:, j * D:(j + 1) * D] for j in js}
        kk = {j: _mm_nt(kh[j], kh[j]) for j in js}
        qk = {j: _mm_nt(qh[j], kh[j]) for j in js}
        st = {i: st_ref[i] for i in hs}
        og, attn, a_o, tp, pw, rhs = {}, {}, {}, {}, {}, {}
        for i in hs:
            j, gi = i // rep, n_v + i
            bcol = beta_t[:, i:i + 1]
            dec = jnp.exp(jnp.where(lower, gc[:, gi:gi + 1] - gc_t[gi:gi + 1, :], -jnp.inf))
            a_kk = jnp.where(strict, bcol * kk[j] * dec, 0.0)
            attn[i] = qk[j] * dec
            og[i] = _mm(qh[j] * e_gc[:, gi:gi + 1], st[i])
            a_d = jnp.where(same_blk, a_kk, 0.0)
            a_o[i] = a_kk - a_d
            tp[i] = -a_d
            pw[i] = tp[i]
            kb = kh[j] * bcol
            rhs[i] = jnp.concatenate([v_ref[:, i * D:(i + 1) * D] * bcol, kb * e_gc[:, gi:gi + 1]],
                                     axis=1)
        n_sq = int(math.log2(GDN_SOLVE_BLOCK)) - 1
        for i in hs:
            pw[i] = _mm(pw[i], pw[i])
        for s in range(n_sq):
            for i in hs:
                tp[i] = tp[i] + pw[i] + _mm(tp[i], pw[i])
            if s + 1 < n_sq:
                for i in hs:
                    pw[i] = _mm(pw[i], pw[i])
        nn = {i: a_o[i] + _mm(tp[i], a_o[i]) for i in hs}
        y = {i: rhs[i] + _mm(tp[i], rhs[i]) for i in hs}
        n2 = {i: _mm(nn[i], nn[i]) for i in hs}
        zz = {i: y[i] + _mm(n2[i], y[i]) for i in hs}
        x = {i: zz[i] - _mm(nn[i], zz[i]) for i in hs}
        v_new = {i: x[i][:, :D] - _mm(x[i][:, D:], st[i]) for i in hs}
        for i in hs:
            j, gi = i // rep, n_v + i
            out = og[i] + _mm(attn[i], v_new[i])
            kd = kh[j] * e_rest[:, gi:gi + 1]
            st_ref[i] = st[i] * e_last[:, gi:gi + 1] + _mm(kd.T, v_new[i])
            ms = jnp.mean(out * out, axis=-1, keepdims=True)
            zh = z_ref[:, i * D:(i + 1) * D]
            o_ref[:, i * D:(i + 1) * D] = (out * lax.rsqrt(ms + NORM_EPS) * nw * _silu(zh)).astype(o_ref.dtype)


def gated_delta(qk, v, z, ba, prm, norm_w, n_qk, n_v):
    S = qk.shape[0]
    D = GDN_HEAD_DIM
    C = GDN_CHUNK
    qw, vw = n_qk * D, n_v * D
    assert GDN_SOLVE_BLOCK * 4 == C
    vmem = 2 * (2 * _nbytes((C, qw), F32) + 2 * _nbytes((C, vw), F32) + _nbytes((C, vw), BF16))
    vmem += _nbytes((n_v, D, D), F32)
    return pl.pallas_call(
        functools.partial(_delta_kernel, n_v=n_v, rep=n_v // n_qk,
                          head_group=math.gcd(n_v, GDN_HEAD_GROUP)),
        grid=(S // C,),
        in_specs=[pl.BlockSpec((C, qw), lambda c: (c, 0)),
                  pl.BlockSpec((C, qw), lambda c: (c, 1)),
                  pl.BlockSpec((C, vw), lambda c: (c, 0)),
                  pl.BlockSpec((C, vw), lambda c: (c, 0)),
                  pl.BlockSpec((C, LANES), lambda c: (c, 0)),
                  pl.BlockSpec((2, LANES), lambda c: (0, 0)),
                  pl.BlockSpec((1, D), lambda c: (0, 0))],
        out_specs=pl.BlockSpec((C, vw), lambda c: (c, 0)),
        out_shape=jax.ShapeDtypeStruct((S, vw), BF16),
        scratch_shapes=[pltpu.VMEM((n_v, D, D), F32)],
        compiler_params=_cparams(1, vmem),
        name="gated_delta",
    )(qk, qk, v, z, ba, prm, norm_w.reshape(1, D))


CONF_HIST = 32


def _conf_kernel(u_ref, w_ref, bdw_ref, g_ref, b_ref, o_ref, ext_ref, sh_ref, y_ref, *, tm, cw,
                 n_c, row_chunk):
    m = pl.program_id(0)
    c = pl.program_id(1)

    @pl.when(m == 0)
    def _():
        ext_ref[c, 0:CONF_HIST, :] = jnp.zeros((CONF_HIST, cw), F32)

    ext_ref[c, CONF_HIST:CONF_HIST + tm, :] = u_ref[...]
    n_rows = CONF_HIST + tm - SUBLANES
    for j in range(1, SUBLANES):
        sh_ref[j - 1, SUBLANES:SUBLANES + n_rows, :] = ext_ref[c, SUBLANES - j:SUBLANES - j + n_rows, :]
    bias = bdw_ref[...]
    for r in range(0, tm, row_chunk):
        y = None
        for k in range(CONF_KERNEL):
            shift = CONF_KERNEL - 1 - k
            a, j = divmod(shift, SUBLANES)
            off = CONF_HIST - SUBLANES * a + r
            if j == 0:
                src = ext_ref[c, off:off + row_chunk, :]
            else:
                src = sh_ref[j - 1, off:off + row_chunk, :]
            term = src * w_ref[k:k + 1, :]
            y = term if y is None else y + term
        y_ref[c, r:r + row_chunk, :] = y + bias
    ext_ref[c, 0:CONF_HIST, :] = ext_ref[c, tm:tm + CONF_HIST, :]

    @pl.when(c == n_c - 1)
    def _():
        d = n_c * cw
        tot = jnp.zeros((tm, 1), F32)
        for j in range(n_c):
            tot = tot + jnp.sum(y_ref[j], axis=-1, keepdims=True)
        mean = tot / d
        var = jnp.zeros((tm, 1), F32)
        for j in range(n_c):
            yc = y_ref[j] - mean
            var = var + jnp.sum(yc * yc, axis=-1, keepdims=True)
        inv = lax.rsqrt(var / d + LN_EPS)
        for j in range(n_c):
            zz = (y_ref[j] - mean) * inv * g_ref[:, j * cw:(j + 1) * cw] + b_ref[:, j * cw:(j + 1) * cw]
            o_ref[:, j * cw:(j + 1) * cw] = _silu(zz).astype(o_ref.dtype)


def conformer_mid(u, w_dw, b_dw, ln_g, ln_b, *, tm=256, cw=512, row_chunk=32):
    S, D = u.shape
    tm, cw = min(tm, S), min(cw, D)
    n_c = D // cw
    vmem = 2 * (_nbytes((tm, cw), F32) + _nbytes((CONF_HIST, cw), F32) + _nbytes((tm, D), BF16))
    vmem += _nbytes((n_c + SUBLANES - 1, tm + CONF_HIST, cw), F32) + 3 * _nbytes((n_c, tm, cw), F32)
    kern = functools.partial(_conf_kernel, tm=tm, cw=cw, n_c=n_c, row_chunk=row_chunk)
    return pl.pallas_call(
        kern,
        grid=(S // tm, n_c),
        in_specs=[pl.BlockSpec((tm, cw), lambda m, c: (m, c)),
                  pl.BlockSpec((CONF_KERNEL, cw), lambda m, c: (0, c)),
                  pl.BlockSpec((1, cw), lambda m, c: (0, c)),
                  pl.BlockSpec((1, D), lambda m, c: (0, 0)),
                  pl.BlockSpec((1, D), lambda m, c: (0, 0))],
        out_specs=pl.BlockSpec((tm, D), lambda m, c: (m, 0)),
        out_shape=jax.ShapeDtypeStruct((S, D), BF16),
        scratch_shapes=[pltpu.VMEM((n_c, tm + CONF_HIST, cw), F32),
                        pltpu.VMEM((SUBLANES - 1, tm + CONF_HIST, cw), F32),
                        pltpu.VMEM((n_c, tm, cw), F32)],
        compiler_params=_cparams(2, vmem),
        name="conformer_mid",
    )(u, w_dw, b_dw.reshape(1, D), ln_g.reshape(1, D), ln_b.reshape(1, D))


def _rope_tables(S):
    half = ATT_HEAD_DIM // 2
    inv = jnp.power(ROPE_THETA, -jnp.arange(half, dtype=F32) / half)
    ang = jnp.arange(S).astype(F32)[:, None] * inv[None, :]
    cos, sin = jnp.cos(ang), jnp.sin(ang)
    return jnp.concatenate([cos, cos], axis=1), jnp.concatenate([-sin, sin], axis=1)


def _wide_tile(n, cap=512):
    t = cap
    while n % t:
        t //= 2
    assert t >= LANES
    return t


PROJ_COLS = 1024


def _out_proj(a, w_o, j, x, gain, bias=None):
    return down_residual(a, w_o.astype(BF16), j, x, gain, bias, tm=1024, tn=_wide_tile(x.shape[1]))


def _mixer_a(hn, x, gain, w_qkv, w_o, sinks, j):
    S, Dm = x.shape
    n_heads = w_o.shape[1] // ATT_HEAD_DIM
    n_kv = n_heads // ATT_GROUP
    n_qkv = w_qkv.shape[2]
    qkv = proj(hn, [(w_qkv, j, 0)], [], "plain", n_qkv, F32, tn=_wide_tile(n_qkv, PROJ_COLS))
    cos_full, sin_signed = _rope_tables(S)
    o = swa_attention(qkv, sinks[j], cos_full, sin_signed, n_heads, n_kv)
    return _out_proj(o, w_o, j, x, gain)


def _mixer_b(hn, x, gain, w_in, conv_w, a_log, dt_bias, norm_w, w_o, j):
    S, Dm = x.shape
    D = GDN_HEAD_DIM
    conv_w, a_log, dt_bias, norm_w = conv_w[j], a_log[j], dt_bias[j], norm_w[j]
    n_v = a_log.shape[0]
    val_dim = n_v * D
    key_dim = (conv_w.shape[1] - val_dim) // 2
    n_qk = key_dim // D
    n_qkv = 2 * key_dim + val_dim
    tn = _wide_tile(key_dim, PROJ_COLS)
    n_taps = conv_w.shape[0]
    qk = proj(hn, [(w_in, j, 0)], [(conv_w, 0)], "conv_silu_l2", 2 * key_dim, F32, tn=tn,
              n_taps=n_taps, q_tiles=key_dim // tn)
    v = proj(hn, [(w_in, j, 2 * key_dim)], [(conv_w, 2 * key_dim)], "conv_silu", val_dim, F32,
             tn=tn, n_taps=n_taps)
    z = proj(hn, [(w_in, j, n_qkv)], [], "plain", val_dim, F32, tn=tn)
    w_ba = jnp.pad(w_in[j][:, n_qkv + val_dim:], ((0, 0), (0, LANES - 2 * n_v)))
    ba = proj(hn, [(w_ba[None], 0, 0)], [], "plain", LANES, F32, tn=LANES)
    prm = jnp.zeros((2, LANES), F32)
    prm = prm.at[0, n_v:2 * n_v].set(a_log).at[1, n_v:2 * n_v].set(dt_bias)
    o = gated_delta(qk, v, z, ba, prm, norm_w, n_qk, n_v)
    return _out_proj(o, w_o, j, x, gain)


def _mixer_c(hn, x, gain, w_pw1, b_pw1, w_dw, b_dw, ln_g, ln_b, w_pw2, b_pw2, j):
    S, Dm = x.shape
    b1 = b_pw1[j].reshape(1, -1)
    u = proj(hn, [(w_pw1, j, 0), (w_pw1, j, Dm)], [(b1, 0), (b1, Dm)], "glu_bias", Dm, F32,
             tn=_wide_tile(Dm, PROJ_COLS // 2))
    mid = conformer_mid(u, w_dw[j], b_dw[j], ln_g[j], ln_b[j])
    return _out_proj(mid, w_pw2, j, x, gain, b_pw2[j])


def _mixer_d(hn, x, gain, w_in, w_conv, w_out, j):
    S, Dm = x.shape
    mid = proj(hn, [(w_in, j, 0), (w_in, j, Dm), (w_in, j, 2 * Dm)], [(w_conv[j], 0)],
               "short_conv", Dm, BF16, tn=256, n_taps=w_conv.shape[1])
    return _out_proj(mid, w_out, j, x, gain)


def _ffn(hn, x, gain, w_gate, w_up, w_conv, b_conv, w_down, i):
    d_ff = w_gate.shape[2]
    mid = proj(hn, [(w_gate, i, 0), (w_up, i, 0)], [(w_conv[i], 0), (b_conv[i].reshape(1, -1), 0)],
               "conv_glu", d_ff, BF16, tn=256, tiles=PROJ_COLS // 512, n_taps=w_conv.shape[1])
    return down_residual(mid, w_down.astype(BF16), i, x, gain)


def kernel(x, mix_norm, ffn_norm, final_norm, a_w_qkv, a_w_o, a_sinks, b_w_in, b_conv, b_a_log, b_dt_bias, b_norm, b_w_o, c_w_pw1, c_b_pw1, c_w_dw, c_b_dw, c_ln_g, c_ln_b, c_w_pw2, c_b_pw2, d_w_in, d_w_conv, d_w_out, f_w_gate, f_w_up, f_w_conv, f_b_conv, f_w_down):
    B, S, Dm = x.shape
    depth = mix_norm.shape[0]
    outs = []
    for bi in range(B):
        xb = x[bi]
        hn = gain_rstd(xb, mix_norm[0])
        for i in range(depth):
            kind, j = i % 4, i // 4
            g_ffn = ffn_norm[i]
            if kind == 0:
                xb, hn = _mixer_a(hn, xb, g_ffn, a_w_qkv, a_w_o, a_sinks, j)
            elif kind == 1:
                xb, hn = _mixer_b(hn, xb, g_ffn, b_w_in, b_conv, b_a_log, b_dt_bias, b_norm, b_w_o, j)
            elif kind == 2:
                xb, hn = _mixer_c(hn, xb, g_ffn, c_w_pw1, c_b_pw1, c_w_dw, c_b_dw, c_ln_g, c_ln_b,
                                  c_w_pw2, c_b_pw2, j)
            else:
                xb, hn = _mixer_d(hn, xb, g_ffn, d_w_in, d_w_conv, d_w_out, j)
            g_next = mix_norm[i + 1] if i + 1 < depth else final_norm
            xb, hn = _ffn(hn, xb, g_next, f_w_gate, f_w_up, f_w_conv, f_b_conv, f_w_down, i)
        outs.append(rmsnorm(xb, final_norm, F32))
    return jnp.stack(outs, axis=0)
```

```python
import functools
import math

import jax
import jax.numpy as jnp
from jax import lax
from jax.experimental import pallas as pl
from jax.experimental.pallas import tpu as pltpu

F32 = jnp.float32
BF16 = jnp.bfloat16

NORM_EPS = 1e-6
LN_EPS = 1e-5
L2_EPS = 1e-6
ROPE_THETA = 10000.0
ATT_HEAD_DIM = 128
ATT_GROUP = 4
ATT_KV_GROUP = 2
WINDOW = 128
GDN_HEAD_DIM = 128
GDN_CHUNK = 64
GDN_SOLVE_BLOCK = 16
GDN_HEAD_GROUP = 32
CONF_KERNEL = 31

LANES = 128
SUBLANES = 8
VMEM_CAP_BYTES = 61 * 1024 * 1024
VMEM_SLACK_BYTES = 14 * 1024 * 1024


def _cparams(n_axes, vmem_bytes):
    limit = int(min(VMEM_CAP_BYTES, max(vmem_bytes + VMEM_SLACK_BYTES, 16 * 1024 * 1024)))
    return pltpu.CompilerParams(dimension_semantics=("arbitrary",) * n_axes, vmem_limit_bytes=limit)


def _nbytes(shape, dtype):
    return math.prod(shape) * jnp.dtype(dtype).itemsize


def _silu(x):
    return x * jax.nn.sigmoid(x)


def _mm(a, b):
    return jnp.dot(a.astype(BF16), b.astype(BF16), preferred_element_type=F32)


def _mm_nt(a, b):
    return lax.dot_general(a.astype(BF16), b.astype(BF16), (((1,), (1,)), ((), ())),
                           preferred_element_type=F32)


def _rmsnorm_kernel(x_ref, g_ref, o_ref):
    x = x_ref[...]
    ms = jnp.mean(x * x, axis=-1, keepdims=True)
    o_ref[...] = (x * lax.rsqrt(ms + NORM_EPS) * g_ref[...]).astype(o_ref.dtype)


def rmsnorm(x, g, out_dtype, tm=512):
    S, D = x.shape
    tm = min(tm, S)
    vmem = 2 * (_nbytes((tm, D), F32) + _nbytes((tm, D), out_dtype)) + _nbytes((tm, D), F32)
    return pl.pallas_call(
        _rmsnorm_kernel,
        grid=(S // tm,),
        in_specs=[pl.BlockSpec((tm, D), lambda m: (m, 0)),
                  pl.BlockSpec((1, D), lambda m: (0, 0))],
        out_specs=pl.BlockSpec((tm, D), lambda m: (m, 0)),
        out_shape=jax.ShapeDtypeStruct((S, D), out_dtype),
        compiler_params=_cparams(1, vmem),
        name="rmsnorm",
    )(x, g.reshape(1, D))


def _gain_rstd_kernel(x_ref, g_ref, xb_ref, rs_ref):
    x = x_ref[...]
    ms = jnp.mean(x * x, axis=-1, keepdims=True)
    xb_ref[...] = (x * g_ref[...]).astype(xb_ref.dtype)
    rs_ref[...] = jnp.broadcast_to(lax.rsqrt(ms + NORM_EPS), rs_ref.shape)


def gain_rstd(x, g, tm=512):
    S, D = x.shape
    tm = min(tm, S)
    vmem = 2 * (_nbytes((tm, D), F32) + _nbytes((tm, D), BF16)) + _nbytes((tm, D), F32)
    return pl.pallas_call(
        _gain_rstd_kernel,
        grid=(S // tm,),
        in_specs=[pl.BlockSpec((tm, D), lambda m: (m, 0)),
                  pl.BlockSpec((1, D), lambda m: (0, 0))],
        out_specs=[pl.BlockSpec((tm, D), lambda m: (m, 0)),
                   pl.BlockSpec((tm, LANES), lambda m: (m, 0))],
        out_shape=[jax.ShapeDtypeStruct((S, D), BF16), jax.ShapeDtypeStruct((S, LANES), F32)],
        compiler_params=_cparams(1, vmem),
        name="gain_rstd",
    )(x, g.reshape(1, D))


def _causal_taps(ext_ref, w_ref, n_taps, r, rows):
    y = None
    for k in range(n_taps):
        off = SUBLANES - (n_taps - 1 - k) + r
        term = ext_ref[off:off + rows, :] * w_ref[k:k + 1, :]
        y = term if y is None else y + term
    return y


class _TiledParam:
    def __init__(self, refs):
        self.refs = refs

    def __getitem__(self, idx):
        parts = [r[idx] for r in self.refs]
        return parts[0] if len(parts) == 1 else jnp.concatenate(parts, axis=-1)


def _proj_kernel(*refs, w_src, n_blocks, n_p, tiles, mode, tm, tn, n_taps, row_block, row_chunk,
                 q_tiles, has_side):
    n_w = len(w_src)
    h_ref, rs_ref = refs[0], refs[1]
    w_hbm = refs[2:2 + n_w]
    p_all = refs[2 + n_w:2 + n_w + n_p * tiles]
    p_refs = [_TiledParam(p_all[k * tiles:(k + 1) * tiles]) for k in range(n_p)]
    rest = list(refs[2 + n_w + n_p * tiles:])
    side_in = rest.pop(0) if has_side else None
    o_ref = rest.pop(0)
    side_out = rest.pop(0) if has_side else None
    stage_ref, sem, wbf_ref = rest[:3]
    ext_ref = rest[3] if n_taps else None
    tb = tn // tiles
    n = pl.program_id(0)
    m = pl.program_id(1)

    def weight_copies(step):
        copies = []
        for i, (layer, first_block) in enumerate(w_src):
            for t in range(tiles):
                blk = jnp.minimum(step * tiles + t, n_blocks - 1) + first_block
                src = w_hbm[i].at[layer, :, pl.ds(pl.multiple_of(blk * tb, tb), tb)]
                k = i * tiles + t
                copies.append(pltpu.make_async_copy(src, stage_ref.at[k], sem.at[k]))
        return copies

    @pl.when(m == 0)
    def _():
        @pl.when(n == 0)
        def _():
            for c in weight_copies(n):
                c.start()

        for k, c in enumerate(weight_copies(n)):
            c.wait()
            wbf_ref[:, k * tb:(k + 1) * tb] = stage_ref[k].astype(BF16)

        @pl.when(n + 1 < pl.num_programs(0))
        def _():
            for c in weight_copies(n + 1):
                c.start()

        if n_taps:
            ext_ref[0:SUBLANES, :] = jnp.zeros((SUBLANES, tn), F32)

    def dots(r0):
        acc = jnp.dot(h_ref[r0:r0 + row_block, :], wbf_ref[...], preferred_element_type=F32)
        rs = rs_ref[r0:r0 + row_block, :]
        rs = jnp.concatenate([rs] * (tn // LANES), axis=1)
        return [acc[:, i * tn:(i + 1) * tn] * rs for i in range(n_w)]

    def to_ext(r0, val):
        ext_ref[SUBLANES + r0:SUBLANES + r0 + row_block, :] = val

    def chunks(r0):
        return [(r, r - r0) for r in range(r0, r0 + row_block, row_chunk)]

    if mode == "plain":
        def matmul(r0):
            return dots(r0)[0]

        def finish(r0, acc):
            o_ref[r0:r0 + row_block, :] = acc.astype(o_ref.dtype)
    elif mode == "glu_bias":
        def matmul(r0):
            return dots(r0)

        def finish(r0, accs):
            val = accs[0] + p_refs[0][...]
            gate = accs[1] + p_refs[1][...]
            o_ref[r0:r0 + row_block, :] = (val * jax.nn.sigmoid(gate)).astype(o_ref.dtype)
    elif mode == "conv_silu":
        def matmul(r0):
            to_ext(r0, dots(r0)[0])

        def finish(r0, _):
            for r, _ in chunks(r0):
                y = _causal_taps(ext_ref, p_refs[0], n_taps, r, row_chunk)
                o_ref[r:r + row_chunk, :] = _silu(y).astype(o_ref.dtype)
    elif mode == "conv_silu_l2":
        scale = jnp.where(n < q_tiles, GDN_HEAD_DIM ** -0.5, 1.0).astype(F32)

        def matmul(r0):
            to_ext(r0, dots(r0)[0])

        def finish(r0, _):
            for r, _ in chunks(r0):
                y = _silu(_causal_taps(ext_ref, p_refs[0], n_taps, r, row_chunk))
                for j in range(0, tn, GDN_HEAD_DIM):
                    yh = y[:, j:j + GDN_HEAD_DIM]
                    inv = lax.rsqrt(jnp.sum(yh * yh, axis=-1, keepdims=True) + L2_EPS)
                    o_ref[r:r + row_chunk, j:j + GDN_HEAD_DIM] = (yh * inv * scale).astype(o_ref.dtype)
    elif mode == "short_conv":
        def matmul(r0):
            bg, cg, xin = dots(r0)
            to_ext(r0, cg * xin)
            return bg

        def finish(r0, bg):
            for r, rr in chunks(r0):
                y = _causal_taps(ext_ref, p_refs[0], n_taps, r, row_chunk)
                o_ref[r:r + row_chunk, :] = (bg[rr:rr + row_chunk, :] * y).astype(o_ref.dtype)
    elif mode == "conv_glu":
        def matmul(r0):
            gate, up = dots(r0)
            to_ext(r0, gate)
            return up

        def finish(r0, up):
            bias = p_refs[1][...]
            for r, rr in chunks(r0):
                y = _causal_taps(ext_ref, p_refs[0], n_taps, r, row_chunk)
                o_ref[r:r + row_chunk, :] = (_silu(y + bias) * up[rr:rr + row_chunk, :]).astype(o_ref.dtype)
    else:
        raise ValueError(mode)

    pending = None
    for r0 in range(0, tm, row_block):
        acc = matmul(r0)
        if pending is not None:
            finish(*pending)
        pending = (r0, acc)
    finish(*pending)
    if n_taps:
        ext_ref[0:SUBLANES, :] = ext_ref[tm:tm + SUBLANES, :]
    if has_side:
        side_out[...] = side_in[...].astype(side_out.dtype)


def _side_rows(n_rows, n_steps):
    bf16_rows = 2 * SUBLANES
    for r in range(bf16_rows, n_rows + 1, bf16_rows):
        if n_rows % r == 0 and n_rows // r <= n_steps:
            return r
    raise ValueError((n_rows, n_steps))


def proj(hn, weights, params, mode, n_out, out_dtype, *, tn, tiles=1, tm=None, n_taps=0, q_tiles=0,
         row_block=512, row_chunk=128, side=None):
    h, rstd = hn
    S, K = h.shape
    if tm is None:
        tm = 1024 if jnp.dtype(out_dtype).itemsize == 2 else 512
    tm = min(tm, S)
    row_block = min(row_block, tm)
    assert S % tm == 0 and n_out % tn == 0 and tm % row_block == 0 and row_block % row_chunk == 0
    n_w = len(weights)
    n_blocks = n_out // tn
    in_specs = [pl.BlockSpec((tm, K), lambda n, m: (m, 0)),
                pl.BlockSpec((tm, LANES), lambda n, m: (m, 0))]
    args = [h, rstd]
    vmem = 2 * (_nbytes((tm, K), h.dtype) + _nbytes((tm, LANES), F32))

    def col_block(n, t, o):
        return jnp.minimum(n * tiles + t, n_blocks - 1) + o

    w_src = []
    for w, layer, off in weights:
        assert w.shape[1] == K and off % tn == 0 and w.dtype == F32
        in_specs.append(pl.BlockSpec(memory_space=pl.ANY))
        args.append(w)
        w_src.append((layer, off // tn))
        vmem += tiles * (_nbytes((K, tn), F32) + _nbytes((K, tn), BF16))
    for p, off in params:
        for t in range(tiles):
            in_specs.append(pl.BlockSpec(
                (p.shape[0], tn), lambda n, m, t=t, o=off // tn: (0, col_block(n, t, o))))
            args.append(p)
        vmem += tiles * 2 * _nbytes((SUBLANES, tn), F32)
    tw = tiles * tn
    scratch = [pltpu.VMEM((n_w * tiles, K, tn), F32),
               pltpu.SemaphoreType.DMA((n_w * tiles,)),
               pltpu.VMEM((K, n_w * tw), BF16)]
    if n_taps:
        scratch.append(pltpu.VMEM((tm + SUBLANES, tw), F32))
        vmem += _nbytes((tm + SUBLANES, tw), F32)
    vmem += 2 * _nbytes((tm, tw), out_dtype) + n_w * _nbytes((row_block, tw), F32)
    n_steps_n, n_steps_m = pl.cdiv(n_blocks, tiles), S // tm
    out_specs = [pl.BlockSpec((tm, tw), lambda n, m: (m, n))]
    out_shape = [jax.ShapeDtypeStruct((S, n_out), out_dtype)]
    if side is not None:
        w_side, side_layer = side
        _, ks, ns = w_side.shape
        rows = _side_rows(ks, n_steps_n * n_steps_m)

        def side_block(n, m):
            return jnp.minimum(n * n_steps_m + m, ks // rows - 1)

        in_specs.append(pl.BlockSpec((None, rows, ns), lambda n, m: (side_layer, side_block(n, m), 0)))
        args.append(w_side)
        out_specs.append(pl.BlockSpec((rows, ns), lambda n, m: (side_block(n, m), 0)))
        out_shape.append(jax.ShapeDtypeStruct((ks, ns), BF16))
        vmem += 2 * (_nbytes((rows, ns), F32) + _nbytes((rows, ns), BF16))
    kern = functools.partial(_proj_kernel, w_src=tuple(w_src), n_blocks=n_blocks, n_p=len(params),
                             tiles=tiles, mode=mode, tm=tm, tn=tw, n_taps=n_taps,
                             row_block=row_block, row_chunk=row_chunk, q_tiles=q_tiles,
                             has_side=side is not None)
    outs = pl.pallas_call(
        kern,
        grid=(n_steps_n, n_steps_m),
        in_specs=in_specs,
        out_specs=out_specs,
        out_shape=out_shape,
        scratch_shapes=scratch,
        compiler_params=_cparams(2, vmem),
        name="proj_" + mode,
    )(*args)
    return outs[0] if side is None else tuple(outs)


def _down_kernel(*refs, has_bias, n_total):
    a_ref, w_ref, x_ref, g_ref = refs[:4]
    b_ref = refs[4] if has_bias else None
    o_ref, xb_ref, rs_ref = refs[-3:]
    n = pl.program_id(1)
    acc = jnp.dot(a_ref[...], w_ref[...], preferred_element_type=F32)
    if has_bias:
        acc = acc + b_ref[...]
    xn = x_ref[...] + acc
    o_ref[...] = xn
    xb_ref[...] = (xn * g_ref[...]).astype(xb_ref.dtype)
    sq = xn * xn
    part = sq[:, 0:LANES]
    for j in range(LANES, sq.shape[1], LANES):
        part = part + sq[:, j:j + LANES]

    @pl.when(n == 0)
    def _():
        rs_ref[...] = part

    @pl.when(n > 0)
    def _():
        rs_ref[...] = rs_ref[...] + part

    @pl.when(n == pl.num_programs(1) - 1)
    def _():
        ms = jnp.sum(rs_ref[...], axis=-1, keepdims=True) / n_total
        rs_ref[...] = jnp.broadcast_to(lax.rsqrt(ms + NORM_EPS), rs_ref.shape)


def down_residual(a, w_bf16, layer, x, gain, bias=None, *, tm=512, tn=512):
    S, K = a.shape
    N = w_bf16.shape[2]
    tm = min(tm, S)
    vmem = 2 * (_nbytes((tm, K), a.dtype) + _nbytes((K, tn), BF16) + 2 * _nbytes((tm, tn), F32)
                + _nbytes((tm, tn), BF16) + _nbytes((tm, LANES), F32))
    vmem += 2 * _nbytes((tm, tn), F32)
    in_specs = [pl.BlockSpec((tm, K), lambda m, n: (m, 0)),
                pl.BlockSpec((None, K, tn), lambda m, n: (layer, 0, n)),
                pl.BlockSpec((tm, tn), lambda m, n: (m, n)),
                pl.BlockSpec((1, tn), lambda m, n: (0, n))]
    args = [a, w_bf16, x, gain.reshape(1, N)]
    if bias is not None:
        in_specs.append(pl.BlockSpec((1, tn), lambda m, n: (0, n)))
        args.append(bias.reshape(1, N))
    x_new, xb, rstd = pl.pallas_call(
        functools.partial(_down_kernel, has_bias=bias is not None, n_total=N),
        grid=(S // tm, N // tn),
        in_specs=in_specs,
        out_specs=[pl.BlockSpec((tm, tn), lambda m, n: (m, n)),
                   pl.BlockSpec((tm, tn), lambda m, n: (m, n)),
                   pl.BlockSpec((tm, LANES), lambda m, n: (m, 0))],
        out_shape=[jax.ShapeDtypeStruct((S, N), F32), jax.ShapeDtypeStruct((S, N), BF16),
                   jax.ShapeDtypeStruct((S, LANES), F32)],
        compiler_params=_cparams(2, vmem),
        name="down_residual",
    )(*args)
    return x_new, (xb, rstd)


def _swa_kernel(sink_ref, q_ref, k_ref, v_ref, cos_ref, sin_ref, o_ref, kprev_ref, vprev_ref,
                *, n_kv, kv_group):
    b = pl.program_id(0)
    D = ATT_HEAD_DIM
    blk = q_ref.shape[0]

    @pl.when(b == 0)
    def _():
        kprev_ref[...] = jnp.zeros(kprev_ref.shape, F32)
        vprev_ref[...] = jnp.zeros(vprev_ref.shape, F32)

    cos = cos_ref[...]
    sin = sin_ref[...]

    def rope(x):
        return x * cos + pltpu.roll(x, D // 2, 1) * sin

    qi = lax.broadcasted_iota(jnp.int32, (blk, 2 * blk), 0)
    kj = lax.broadcasted_iota(jnp.int32, (blk, 2 * blk), 1)
    rel = blk + qi - kj
    valid = (rel >= 0) & (rel < WINDOW) & ((kj >= blk) | (b > 0))
    scale = D ** -0.5

    for g0 in range(0, n_kv, kv_group):
        gs = range(g0, g0 + kv_group)
        kband, vband, s, p = {}, {}, {}, {}
        for g in gs:
            kc = rope(k_ref[:, g * D:(g + 1) * D])
            vc = v_ref[:, g * D:(g + 1) * D]
            kband[g] = jnp.concatenate([kprev_ref[:, g * D:(g + 1) * D], kc], axis=0).astype(BF16)
            vband[g] = jnp.concatenate([vprev_ref[:, g * D:(g + 1) * D], vc], axis=0).astype(BF16)
            kprev_ref[:, g * D:(g + 1) * D] = kc
            vprev_ref[:, g * D:(g + 1) * D] = vc
        hs = [(g, g * ATT_GROUP + gg) for g in gs for gg in range(ATT_GROUP)]
        for g, hh in hs:
            qh = rope(q_ref[:, hh * D:(hh + 1) * D])
            s[hh] = _mm_nt(qh, kband[g]) * scale
        for g, hh in hs:
            sh = jnp.where(valid, s[hh], -jnp.inf)
            sink = sink_ref[hh]
            mx = jnp.maximum(jnp.max(sh, axis=-1, keepdims=True), sink)
            e = jnp.exp(sh - mx)
            denom = jnp.sum(e, axis=-1, keepdims=True) + jnp.exp(sink - mx)
            p[hh] = e / denom
        for g, hh in hs:
            o_ref[:, hh * D:(hh + 1) * D] = _mm(p[hh], vband[g]).astype(o_ref.dtype)


def swa_attention(qkv, sinks, cos_full, sin_signed, n_heads, n_kv):
    S = qkv.shape[0]
    D = ATT_HEAD_DIM
    blk = WINDOW
    qw, kw = n_heads * D, n_kv * D
    vmem = 2 * (_nbytes((blk, qw), F32) + 2 * _nbytes((blk, kw), F32) + 2 * _nbytes((blk, D), F32)
                + _nbytes((blk, qw), BF16)) + 2 * _nbytes((blk, kw), F32)
    return pl.pallas_call(
        functools.partial(_swa_kernel, n_kv=n_kv, kv_group=math.gcd(n_kv, ATT_KV_GROUP)),
        grid=(S // blk,),
        in_specs=[pl.BlockSpec(memory_space=pltpu.SMEM),
                  pl.BlockSpec((blk, qw), lambda b: (b, 0)),
                  pl.BlockSpec((blk, kw), lambda b: (b, qw // kw)),
                  pl.BlockSpec((blk, kw), lambda b: (b, qw // kw + 1)),
                  pl.BlockSpec((blk, D), lambda b: (b, 0)),
                  pl.BlockSpec((blk, D), lambda b: (b, 0))],
        out_specs=pl.BlockSpec((blk, qw), lambda b: (b, 0)),
        out_shape=jax.ShapeDtypeStruct((S, qw), BF16),
        scratch_shapes=[pltpu.VMEM((blk, kw), F32), pltpu.VMEM((blk, kw), F32)],
        compiler_params=_cparams(1, vmem),
        name="swa_attention",
    )(sinks, qkv, qkv, qkv, cos_full, sin_signed)


def _delta_kernel(q_ref, k_ref, v_ref, z_ref, ba_ref, prm_ref, nw_ref, o_ref, st_ref,
                  *, n_v, rep, head_group):
    c = pl.program_id(0)
    C = q_ref.shape[0]
    D = GDN_HEAD_DIM

    @pl.when(c == 0)
    def _():
        st_ref[...] = jnp.zeros(st_ref.shape, F32)

    ba = ba_ref[...]
    beta_t = jax.nn.sigmoid(ba)
    a_sh = ba + prm_ref[1:2, :]
    softplus = jnp.maximum(a_sh, 0.0) + jnp.log(1.0 + jnp.exp(-jnp.abs(a_sh)))
    g_t = -jnp.exp(prm_ref[0:1, :]) * softplus
    row = lax.broadcasted_iota(jnp.int32, (C, LANES), 0)
    gc = g_t
    s = 1
    while s < C:
        gc = gc + jnp.where(row >= s, pltpu.roll(gc, s, 0), 0.0)
        s *= 2
    glast = gc[C - 1:C, :]
    e_gc = jnp.exp(gc)
    e_rest = jnp.exp(glast - gc)
    e_last = jnp.exp(glast)
    gc_t = gc.T

    ii = lax.broadcasted_iota(jnp.int32, (C, C), 0)
    jj = lax.broadcasted_iota(jnp.int32, (C, C), 1)
    lower = ii >= jj
    strict = ii > jj
    same_blk = (ii // GDN_SOLVE_BLOCK) == (jj // GDN_SOLVE_BLOCK)
    nw = nw_ref[...]

    for g0 in range(0, n_v, head_group):
        hs = list(range(g0, g0 + head_group))
        js = sorted({i // rep for i in hs})
        kh = {j: k_ref[:, j * D:(j + 1) * D] for j in js}
        qh = {j: q_ref[:, j * D:(j + 1) * D] for j in js}
        kk = {j: _mm_nt(kh[j], kh[j]) for j in js}
        qk = {j: _mm_nt(qh[j], kh[j]) for j in js}
        st = {i: st_ref[i] for i in hs}
        qg, attn, a_o, tp, pw, rhs = {}, {}, {}, {}, {}, {}
        for i in hs:
            j, gi = i // rep, n_v + i
            bcol = beta_t[:, i:i + 1]
            dec = jnp.exp(jnp.where(lower, gc[:, gi:gi + 1] - gc_t[gi:gi + 1, :], -jnp.inf))
            a_kk = jnp.where(strict, bcol * kk[j] * dec, 0.0)
            attn[i] = qk[j] * dec
            qg[i] = qh[j] * e_gc[:, gi:gi + 1]
            a_d = jnp.where(same_blk, a_kk, 0.0)
            a_o[i] = a_kk - a_d
            tp[i] = -a_d
            pw[i] = tp[i]
            kb = kh[j] * bcol
            rhs[i] = jnp.concatenate([v_ref[:, i * D:(i + 1) * D] * bcol, kb * e_gc[:, gi:gi + 1]],
                                     axis=1)
        n_sq = int(math.log2(GDN_SOLVE_BLOCK)) - 1
        for i in hs:
            pw[i] = _mm(pw[i], pw[i])
        for s in range(n_sq):
            for i in hs:
                tp[i] = tp[i] + pw[i] + _mm(tp[i], pw[i])
            if s + 1 < n_sq:
                for i in hs:
                    pw[i] = _mm(pw[i], pw[i])
        og = {i: _mm(qg[i], st[i]) for i in hs}
        nn = {i: a_o[i] + _mm(tp[i], a_o[i]) for i in hs}
        y = {i: rhs[i] + _mm(tp[i], rhs[i]) for i in hs}
        n2 = {i: _mm(nn[i], nn[i]) for i in hs}
        dd = {i: y[i] - _mm(nn[i], y[i]) for i in hs}
        x = {i: dd[i] + _mm(n2[i], dd[i]) for i in hs}
        v_new = {i: x[i][:, :D] - _mm(x[i][:, D:], st[i]) for i in hs}
        for i in hs:
            j, gi = i // rep, n_v + i
            out = og[i] + _mm(attn[i], v_new[i])
            kd = kh[j] * e_rest[:, gi:gi + 1]
            st_ref[i] = st[i] * e_last[:, gi:gi + 1] + _mm(kd.T, v_new[i])
            ms = jnp.mean(out * out, axis=-1, keepdims=True)
            zh = z_ref[:, i * D:(i + 1) * D]
            o_ref[:, i * D:(i + 1) * D] = (out * lax.rsqrt(ms + NORM_EPS) * nw * _silu(zh)).astype(o_ref.dtype)


def gated_delta(qk, v, z, ba, prm, norm_w, n_qk, n_v):
    S = qk.shape[0]
    D = GDN_HEAD_DIM
    C = GDN_CHUNK
    qw, vw = n_qk * D, n_v * D
    assert GDN_SOLVE_BLOCK * 4 == C
    vmem = 2 * (2 * _nbytes((C, qw), F32) + 2 * _nbytes((C, vw), F32) + _nbytes((C, vw), BF16))
    vmem += _nbytes((n_v, D, D), F32)
    return pl.pallas_call(
        functools.partial(_delta_kernel, n_v=n_v, rep=n_v // n_qk,
                          head_group=math.gcd(n_v, GDN_HEAD_GROUP)),
        grid=(S // C,),
        in_specs=[pl.BlockSpec((C, qw), lambda c: (c, 0)),
                  pl.BlockSpec((C, qw), lambda c: (c, 1)),
                  pl.BlockSpec((C, vw), lambda c: (c, 0)),
                  pl.BlockSpec((C, vw), lambda c: (c, 0)),
                  pl.BlockSpec((C, LANES), lambda c: (c, 0)),
                  pl.BlockSpec((2, LANES), lambda c: (0, 0)),
                  pl.BlockSpec((1, D), lambda c: (0, 0))],
        out_specs=pl.BlockSpec((C, vw), lambda c: (c, 0)),
        out_shape=jax.ShapeDtypeStruct((S, vw), BF16),
        scratch_shapes=[pltpu.VMEM((n_v, D, D), F32)],
        compiler_params=_cparams(1, vmem),
        name="gated_delta",
    )(qk, qk, v, z, ba, prm, norm_w.reshape(1, D))


CONF_HIST = 32


def _conf_kernel(u_ref, w_ref, bdw_ref, g_ref, b_ref, o_ref, ext_ref, sh_ref, y_ref, *, tm, cw,
                 n_c, row_chunk):
    m = pl.program_id(0)
    c = pl.program_id(1)

    @pl.when(m == 0)
    def _():
        ext_ref[c, 0:CONF_HIST, :] = jnp.zeros((CONF_HIST, cw), F32)

    ext_ref[c, CONF_HIST:CONF_HIST + tm, :] = u_ref[...]
    n_rows = CONF_HIST + tm - SUBLANES
    for j in range(1, SUBLANES):
        sh_ref[j - 1, SUBLANES:SUBLANES + n_rows, :] = ext_ref[c, SUBLANES - j:SUBLANES - j + n_rows, :]
    bias = bdw_ref[...]
    for r in range(0, tm, row_chunk):
        y = None
        for k in range(CONF_KERNEL):
            shift = CONF_KERNEL - 1 - k
            a, j = divmod(shift, SUBLANES)
            off = CONF_HIST - SUBLANES * a + r
            if j == 0:
                src = ext_ref[c, off:off + row_chunk, :]
            else:
                src = sh_ref[j - 1, off:off + row_chunk, :]
            term = src * w_ref[k:k + 1, :]
            y = term if y is None else y + term
        y_ref[c, r:r + row_chunk, :] = y + bias
    ext_ref[c, 0:CONF_HIST, :] = ext_ref[c, tm:tm + CONF_HIST, :]

    @pl.when(c == n_c - 1)
    def _():
        d = n_c * cw
        tot = jnp.zeros((tm, 1), F32)
        for j in range(n_c):
            tot = tot + jnp.sum(y_ref[j], axis=-1, keepdims=True)
        mean = tot / d
        var = jnp.zeros((tm, 1), F32)
        for j in range(n_c):
            yc = y_ref[j] - mean
            var = var + jnp.sum(yc * yc, axis=-1, keepdims=True)
        inv = lax.rsqrt(var / d + LN_EPS)
        for j in range(n_c):
            zz = (y_ref[j] - mean) * inv * g_ref[:, j * cw:(j + 1) * cw] + b_ref[:, j * cw:(j + 1) * cw]
            o_ref[:, j * cw:(j + 1) * cw] = _silu(zz).astype(o_ref.dtype)


def conformer_mid(u, w_dw, b_dw, ln_g, ln_b, *, tm=512, cw=512, row_chunk=32):
    S, D = u.shape
    tm, cw = min(tm, S), min(cw, D)
    n_c = D // cw
    vmem = 2 * (_nbytes((tm, cw), F32) + _nbytes((CONF_HIST, cw), F32) + _nbytes((tm, D), BF16))
    vmem += _nbytes((n_c + SUBLANES - 1, tm + CONF_HIST, cw), F32) + 3 * _nbytes((n_c, tm, cw), F32)
    kern = functools.partial(_conf_kernel, tm=tm, cw=cw, n_c=n_c, row_chunk=row_chunk)
    return pl.pallas_call(
        kern,
        grid=(S // tm, n_c),
        in_specs=[pl.BlockSpec((tm, cw), lambda m, c: (m, c)),
                  pl.BlockSpec((CONF_KERNEL, cw), lambda m, c: (0, c)),
                  pl.BlockSpec((1, cw), lambda m, c: (0, c)),
                  pl.BlockSpec((1, D), lambda m, c: (0, 0)),
                  pl.BlockSpec((1, D), lambda m, c: (0, 0))],
        out_specs=pl.BlockSpec((tm, D), lambda m, c: (m, 0)),
        out_shape=jax.ShapeDtypeStruct((S, D), BF16),
        scratch_shapes=[pltpu.VMEM((n_c, tm + CONF_HIST, cw), F32),
                        pltpu.VMEM((SUBLANES - 1, tm + CONF_HIST, cw), F32),
                        pltpu.VMEM((n_c, tm, cw), F32)],
        compiler_params=_cparams(2, vmem),
        name="conformer_mid",
    )(u, w_dw, b_dw.reshape(1, D), ln_g.reshape(1, D), ln_b.reshape(1, D))


def _rope_tables(S):
    half = ATT_HEAD_DIM // 2
    inv = jnp.power(ROPE_THETA, -jnp.arange(half, dtype=F32) / half)
    ang = jnp.arange(S).astype(F32)[:, None] * inv[None, :]
    cos, sin = jnp.cos(ang), jnp.sin(ang)
    return jnp.concatenate([cos, cos], axis=1), jnp.concatenate([-sin, sin], axis=1)


def _wide_tile(n, cap=512):
    t = cap
    while n % t:
        t //= 2
    assert t >= LANES
    return t


PROJ_COLS = 1024


def _out_proj(a, w_o_bf16, x, gain, bias=None):
    return down_residual(a, w_o_bf16[None], 0, x, gain, bias, tm=1024, tn=_wide_tile(x.shape[1]))


def _mixer_a(hn, x, gain, w_qkv, w_o, sinks, j):
    S, Dm = x.shape
    n_heads = w_o.shape[1] // ATT_HEAD_DIM
    n_kv = n_heads // ATT_GROUP
    n_qkv = w_qkv.shape[2]
    qkv, w_o_bf16 = proj(hn, [(w_qkv, j, 0)], [], "plain", n_qkv, F32,
                         tn=_wide_tile(n_qkv, PROJ_COLS), side=(w_o, j))
    cos_full, sin_signed = _rope_tables(S)
    o = swa_attention(qkv, sinks[j], cos_full, sin_signed, n_heads, n_kv)
    return _out_proj(o, w_o_bf16, x, gain)


def _mixer_b(hn, x, gain, w_in, conv_w, a_log, dt_bias, norm_w, w_o, j):
    S, Dm = x.shape
    D = GDN_HEAD_DIM
    conv_w, a_log, dt_bias, norm_w = conv_w[j], a_log[j], dt_bias[j], norm_w[j]
    n_v = a_log.shape[0]
    val_dim = n_v * D
    key_dim = (conv_w.shape[1] - val_dim) // 2
    n_qk = key_dim // D
    n_qkv = 2 * key_dim + val_dim
    tn = _wide_tile(key_dim, PROJ_COLS)
    n_taps = conv_w.shape[0]
    qk, w_o_bf16 = proj(hn, [(w_in, j, 0)], [(conv_w, 0)], "conv_silu_l2", 2 * key_dim, F32, tn=tn,
                        n_taps=n_taps, q_tiles=key_dim // tn, row_block=128, side=(w_o, j))
    v = proj(hn, [(w_in, j, 2 * key_dim)], [(conv_w, 2 * key_dim)], "conv_silu", val_dim, F32,
             tn=tn, n_taps=n_taps, row_block=128)
    z = proj(hn, [(w_in, j, n_qkv)], [], "plain", val_dim, F32, tn=tn)
    w_ba = jnp.pad(w_in[j][:, n_qkv + val_dim:], ((0, 0), (0, LANES - 2 * n_v)))
    ba = proj(hn, [(w_ba[None], 0, 0)], [], "plain", LANES, F32, tn=LANES)
    prm = jnp.zeros((2, LANES), F32)
    prm = prm.at[0, n_v:2 * n_v].set(a_log).at[1, n_v:2 * n_v].set(dt_bias)
    o = gated_delta(qk, v, z, ba, prm, norm_w, n_qk, n_v)
    return _out_proj(o, w_o_bf16, x, gain)


def _mixer_c(hn, x, gain, w_pw1, b_pw1, w_dw, b_dw, ln_g, ln_b, w_pw2, b_pw2, j):
    S, Dm = x.shape
    b1 = b_pw1[j].reshape(1, -1)
    u, w_pw2_bf16 = proj(hn, [(w_pw1, j, 0), (w_pw1, j, Dm)], [(b1, 0), (b1, Dm)], "glu_bias", Dm,
                         F32, tn=_wide_tile(Dm, PROJ_COLS // 2), side=(w_pw2, j))
    mid = conformer_mid(u, w_dw[j], b_dw[j], ln_g[j], ln_b[j])
    return _out_proj(mid, w_pw2_bf16, x, gain, b_pw2[j])


def _mixer_d(hn, x, gain, w_in, w_conv, w_out, j):
    S, Dm = x.shape
    mid, w_out_bf16 = proj(hn, [(w_in, j, 0), (w_in, j, Dm), (w_in, j, 2 * Dm)], [(w_conv[j], 0)],
                           "short_conv", Dm, BF16, tn=256, n_taps=w_conv.shape[1],
                           side=(w_out, j))
    return _out_proj(mid, w_out_bf16, x, gain)


def _ffn(hn, x, gain, w_gate, w_up, w_conv, b_conv, w_down, i):
    d_ff = w_gate.shape[2]
    mid, w_down_bf16 = proj(hn, [(w_gate, i, 0), (w_up, i, 0)],
                            [(w_conv[i], 0), (b_conv[i].reshape(1, -1), 0)], "conv_glu", d_ff, BF16,
                            tn=256, tiles=PROJ_COLS // 512, n_taps=w_conv.shape[1],
                            side=(w_down, i))
    return down_residual(mid, w_down_bf16[None], 0, x, gain)


def kernel(x, mix_norm, ffn_norm, final_norm, a_w_qkv, a_w_o, a_sinks, b_w_in, b_conv, b_a_log, b_dt_bias, b_norm, b_w_o, c_w_pw1, c_b_pw1, c_w_dw, c_b_dw, c_ln_g, c_ln_b, c_w_pw2, c_b_pw2, d_w_in, d_w_conv, d_w_out, f_w_gate, f_w_up, f_w_conv, f_b_conv, f_w_down):
    B, S, Dm = x.shape
    depth = mix_norm.shape[0]
    outs = []
    for bi in range(B):
        xb = x[bi]
        hn = gain_rstd(xb, mix_norm[0])
        for i in range(depth):
            kind, j = i % 4, i // 4
            g_ffn = ffn_norm[i]
            if kind == 0:
                xb, hn = _mixer_a(hn, xb, g_ffn, a_w_qkv, a_w_o, a_sinks, j)
            elif kind == 1:
                xb, hn = _mixer_b(hn, xb, g_ffn, b_w_in, b_conv, b_a_log, b_dt_bias, b_norm, b_w_o, j)
            elif kind == 2:
                xb, hn = _mixer_c(hn, xb, g_ffn, c_w_pw1, c_b_pw1, c_w_dw, c_b_dw, c_ln_g, c_ln_b,
                                  c_w_pw2, c_b_pw2, j)
            else:
                xb, hn = _mixer_d(hn, xb, g_ffn, d_w_in, d_w_conv, d_w_out, j)
            g_next = mix_norm[i + 1] if i + 1 < depth else final_norm
            xb, hn = _ffn(hn, xb, g_next, f_w_gate, f_w_up, f_w_conv, f_b_conv, f_w_down, i)
        outs.append(rmsnorm(xb, final_norm, F32))
    return jnp.stack(outs, axis=0)
```

```python
import functools
import math

import jax
import jax.numpy as jnp
from jax import lax
from jax.experimental import pallas as pl
from jax.experimental.pallas import tpu as pltpu

F32 = jnp.float32
BF16 = jnp.bfloat16

NORM_EPS = 1e-6
LN_EPS = 1e-5
L2_EPS = 1e-6
ROPE_THETA = 10000.0
ATT_HEAD_DIM = 128
ATT_GROUP = 4
ATT_KV_GROUP = 2
WINDOW = 128
GDN_HEAD_DIM = 128
GDN_CHUNK = 64
GDN_SOLVE_BLOCK = 16
GDN_HEAD_GROUP = 32
CONF_KERNEL = 31

LANES = 128
SUBLANES = 8
VMEM_CAP_BYTES = 61 * 1024 * 1024
VMEM_SLACK_BYTES = 14 * 1024 * 1024


def _cparams(n_axes, vmem_bytes):
    limit = int(min(VMEM_CAP_BYTES, max(vmem_bytes + VMEM_SLACK_BYTES, 16 * 1024 * 1024)))
    return pltpu.CompilerParams(dimension_semantics=("arbitrary",) * n_axes, vmem_limit_bytes=limit)


def _nbytes(shape, dtype):
    return math.prod(shape) * jnp.dtype(dtype).itemsize


def _silu(x):
    return x * jax.nn.sigmoid(x)


def _mm(a, b):
    return jnp.dot(a.astype(BF16), b.astype(BF16), preferred_element_type=F32)


def _mm_nt(a, b):
    return lax.dot_general(a.astype(BF16), b.astype(BF16), (((1,), (1,)), ((), ())),
                           preferred_element_type=F32)


def _rmsnorm_kernel(x_ref, g_ref, o_ref):
    x = x_ref[...]
    ms = jnp.mean(x * x, axis=-1, keepdims=True)
    o_ref[...] = (x * lax.rsqrt(ms + NORM_EPS) * g_ref[...]).astype(o_ref.dtype)


def rmsnorm(x, g, out_dtype, tm=512):
    S, D = x.shape
    tm = min(tm, S)
    vmem = 2 * (_nbytes((tm, D), F32) + _nbytes((tm, D), out_dtype)) + _nbytes((tm, D), F32)
    return pl.pallas_call(
        _rmsnorm_kernel,
        grid=(S // tm,),
        in_specs=[pl.BlockSpec((tm, D), lambda m: (m, 0)),
                  pl.BlockSpec((1, D), lambda m: (0, 0))],
        out_specs=pl.BlockSpec((tm, D), lambda m: (m, 0)),
        out_shape=jax.ShapeDtypeStruct((S, D), out_dtype),
        compiler_params=_cparams(1, vmem),
        name="rmsnorm",
    )(x, g.reshape(1, D))


def _gain_rstd_kernel(x_ref, g_ref, xb_ref, rs_ref):
    x = x_ref[...]
    ms = jnp.mean(x * x, axis=-1, keepdims=True)
    xb_ref[...] = (x * g_ref[...]).astype(xb_ref.dtype)
    rs_ref[...] = jnp.broadcast_to(lax.rsqrt(ms + NORM_EPS), rs_ref.shape)


def gain_rstd(x, g, tm=512):
    S, D = x.shape
    tm = min(tm, S)
    vmem = 2 * (_nbytes((tm, D), F32) + _nbytes((tm, D), BF16)) + _nbytes((tm, D), F32)
    return pl.pallas_call(
        _gain_rstd_kernel,
        grid=(S // tm,),
        in_specs=[pl.BlockSpec((tm, D), lambda m: (m, 0)),
                  pl.BlockSpec((1, D), lambda m: (0, 0))],
        out_specs=[pl.BlockSpec((tm, D), lambda m: (m, 0)),
                   pl.BlockSpec((tm, LANES), lambda m: (m, 0))],
        out_shape=[jax.ShapeDtypeStruct((S, D), BF16), jax.ShapeDtypeStruct((S, LANES), F32)],
        compiler_params=_cparams(1, vmem),
        name="gain_rstd",
    )(x, g.reshape(1, D))


def _causal_taps(ext_ref, w_ref, n_taps, r, rows):
    y = None
    for k in range(n_taps):
        off = SUBLANES - (n_taps - 1 - k) + r
        term = ext_ref[off:off + rows, :] * w_ref[k:k + 1, :]
        y = term if y is None else y + term
    return y


class _TiledParam:
    def __init__(self, refs):
        self.refs = refs

    def __getitem__(self, idx):
        parts = [r[idx] for r in self.refs]
        return parts[0] if len(parts) == 1 else jnp.concatenate(parts, axis=-1)


def _proj_kernel(*refs, w_src, w_transposed, n_blocks, n_p, tiles, mode, tm, tn, n_taps, row_block,
                 row_chunk, q_tiles, has_side):
    n_w = len(w_src)
    h_ref, rs_ref = refs[0], refs[1]
    w_hbm = refs[2:2 + n_w]
    p_all = refs[2 + n_w:2 + n_w + n_p * tiles]
    p_refs = [_TiledParam(p_all[k * tiles:(k + 1) * tiles]) for k in range(n_p)]
    rest = list(refs[2 + n_w + n_p * tiles:])
    side_in = rest.pop(0) if has_side else None
    o_ref = rest.pop(0)
    side_out = rest.pop(0) if has_side else None
    stage_ref, sem, wbf_ref = rest[:3]
    ext_ref = rest[3] if n_taps else None
    tb = tn // tiles
    n = pl.program_id(0)
    m = pl.program_id(1)

    def weight_copies(step):
        copies = []
        for i, (layer, first_block) in enumerate(w_src):
            for t in range(tiles):
                blk = jnp.minimum(step * tiles + t, n_blocks - 1) + first_block
                cols = pl.ds(pl.multiple_of(blk * tb, tb), tb)
                src = w_hbm[i].at[layer, cols, :] if w_transposed else w_hbm[i].at[layer, :, cols]
                k = i * tiles + t
                copies.append(pltpu.make_async_copy(src, stage_ref.at[k], sem.at[k]))
        return copies

    @pl.when(m == 0)
    def _():
        @pl.when(n == 0)
        def _():
            for c in weight_copies(n):
                c.start()

        for k, c in enumerate(weight_copies(n)):
            c.wait()
            if w_transposed:
                wbf_ref[k * tb:(k + 1) * tb, :] = stage_ref[k].astype(BF16)
            else:
                wbf_ref[:, k * tb:(k + 1) * tb] = stage_ref[k].astype(BF16)

        @pl.when(n + 1 < pl.num_programs(0))
        def _():
            for c in weight_copies(n + 1):
                c.start()

        if n_taps:
            ext_ref[0:SUBLANES, :] = jnp.zeros((SUBLANES, tn), F32)

    def dots(r0):
        if w_transposed:
            acc = lax.dot_general(h_ref[r0:r0 + row_block, :], wbf_ref[...],
                                  (((1,), (1,)), ((), ())), preferred_element_type=F32)
        else:
            acc = jnp.dot(h_ref[r0:r0 + row_block, :], wbf_ref[...], preferred_element_type=F32)
        rs = rs_ref[r0:r0 + row_block, :]
        rs = jnp.concatenate([rs] * (tn // LANES), axis=1)
        return [acc[:, i * tn:(i + 1) * tn] * rs for i in range(n_w)]

    def to_ext(r0, val):
        ext_ref[SUBLANES + r0:SUBLANES + r0 + row_block, :] = val

    def chunks(r0):
        return [(r, r - r0) for r in range(r0, r0 + row_block, row_chunk)]

    if mode == "plain":
        def matmul(r0):
            return dots(r0)[0]

        def finish(r0, acc):
            o_ref[r0:r0 + row_block, :] = acc.astype(o_ref.dtype)
    elif mode == "glu_bias":
        def matmul(r0):
            return dots(r0)

        def finish(r0, accs):
            val = accs[0] + p_refs[0][...]
            gate = accs[1] + p_refs[1][...]
            o_ref[r0:r0 + row_block, :] = (val * jax.nn.sigmoid(gate)).astype(o_ref.dtype)
    elif mode == "conv_silu":
        def matmul(r0):
            to_ext(r0, dots(r0)[0])

        def finish(r0, _):
            for r, _ in chunks(r0):
                y = _causal_taps(ext_ref, p_refs[0], n_taps, r, row_chunk)
                o_ref[r:r + row_chunk, :] = _silu(y).astype(o_ref.dtype)
    elif mode == "conv_silu_l2":
        scale = jnp.where(n < q_tiles, GDN_HEAD_DIM ** -0.5, 1.0).astype(F32)

        def matmul(r0):
            to_ext(r0, dots(r0)[0])

        def finish(r0, _):
            for r, _ in chunks(r0):
                y = _silu(_causal_taps(ext_ref, p_refs[0], n_taps, r, row_chunk))
                for j in range(0, tn, GDN_HEAD_DIM):
                    yh = y[:, j:j + GDN_HEAD_DIM]
                    inv = lax.rsqrt(jnp.sum(yh * yh, axis=-1, keepdims=True) + L2_EPS)
                    o_ref[r:r + row_chunk, j:j + GDN_HEAD_DIM] = (yh * inv * scale).astype(o_ref.dtype)
    elif mode == "short_conv":
        def matmul(r0):
            bg, cg, xin = dots(r0)
            to_ext(r0, cg * xin)
            return bg

        def finish(r0, bg):
            for r, rr in chunks(r0):
                y = _causal_taps(ext_ref, p_refs[0], n_taps, r, row_chunk)
                o_ref[r:r + row_chunk, :] = (bg[rr:rr + row_chunk, :] * y).astype(o_ref.dtype)
    elif mode == "conv_glu":
        def matmul(r0):
            gate, up = dots(r0)
            to_ext(r0, gate)
            return up

        def finish(r0, up):
            bias = p_refs[1][...]
            for r, rr in chunks(r0):
                y = _causal_taps(ext_ref, p_refs[0], n_taps, r, row_chunk)
                o_ref[r:r + row_chunk, :] = (_silu(y + bias) * up[rr:rr + row_chunk, :]).astype(o_ref.dtype)
    else:
        raise ValueError(mode)

    pending = None
    for r0 in range(0, tm, row_block):
        acc = matmul(r0)
        if pending is not None:
            finish(*pending)
        pending = (r0, acc)
    finish(*pending)
    if n_taps:
        ext_ref[0:SUBLANES, :] = ext_ref[tm:tm + SUBLANES, :]
    if has_side:
        side_out[...] = side_in[...].astype(side_out.dtype)


def _side_rows(n_rows, n_steps):
    bf16_rows = 2 * SUBLANES
    for r in range(bf16_rows, n_rows + 1, bf16_rows):
        if n_rows % r == 0 and n_rows // r <= n_steps:
            return r
    raise ValueError((n_rows, n_steps))


def proj(hn, weights, params, mode, n_out, out_dtype, *, tn, tiles=1, tm=None, n_taps=0, q_tiles=0,
         row_block=512, row_chunk=128, side=None, w_transposed=False):
    h, rstd = hn
    S, K = h.shape
    if tm is None:
        tm = 1024 if jnp.dtype(out_dtype).itemsize == 2 else 512
    tm = min(tm, S)
    row_block = min(row_block, tm)
    assert S % tm == 0 and n_out % tn == 0 and tm % row_block == 0 and row_block % row_chunk == 0
    n_w = len(weights)
    n_blocks = n_out // tn
    in_specs = [pl.BlockSpec((tm, K), lambda n, m: (m, 0)),
                pl.BlockSpec((tm, LANES), lambda n, m: (m, 0))]
    args = [h, rstd]
    vmem = 2 * (_nbytes((tm, K), h.dtype) + _nbytes((tm, LANES), F32))

    def col_block(n, t, o):
        return jnp.minimum(n * tiles + t, n_blocks - 1) + o

    w_src = []
    for w, layer, off in weights:
        assert w.shape[2 if w_transposed else 1] == K and off % tn == 0 and w.dtype == F32
        in_specs.append(pl.BlockSpec(memory_space=pl.ANY))
        args.append(w)
        w_src.append((layer, off // tn))
        vmem += tiles * (_nbytes((K, tn), F32) + _nbytes((K, tn), BF16))
    for p, off in params:
        for t in range(tiles):
            in_specs.append(pl.BlockSpec(
                (p.shape[0], tn), lambda n, m, t=t, o=off // tn: (0, col_block(n, t, o))))
            args.append(p)
        vmem += tiles * 2 * _nbytes((SUBLANES, tn), F32)
    tw = tiles * tn
    scratch = [pltpu.VMEM((n_w * tiles, tn, K) if w_transposed else (n_w * tiles, K, tn), F32),
               pltpu.SemaphoreType.DMA((n_w * tiles,)),
               pltpu.VMEM((n_w * tw, K) if w_transposed else (K, n_w * tw), BF16)]
    if n_taps:
        scratch.append(pltpu.VMEM((tm + SUBLANES, tw), F32))
        vmem += _nbytes((tm + SUBLANES, tw), F32)
    vmem += 2 * _nbytes((tm, tw), out_dtype) + n_w * _nbytes((row_block, tw), F32)
    n_steps_n, n_steps_m = pl.cdiv(n_blocks, tiles), S // tm
    out_specs = [pl.BlockSpec((tm, tw), lambda n, m: (m, n))]
    out_shape = [jax.ShapeDtypeStruct((S, n_out), out_dtype)]
    if side is not None:
        w_side, side_layer = side
        _, ks, ns = w_side.shape
        rows = _side_rows(ks, n_steps_n * n_steps_m)

        def side_block(n, m):
            return jnp.minimum(n * n_steps_m + m, ks // rows - 1)

        in_specs.append(pl.BlockSpec((None, rows, ns), lambda n, m: (side_layer, side_block(n, m), 0)))
        args.append(w_side)
        out_specs.append(pl.BlockSpec((rows, ns), lambda n, m: (side_block(n, m), 0)))
        out_shape.append(jax.ShapeDtypeStruct((ks, ns), BF16))
        vmem += 2 * (_nbytes((rows, ns), F32) + _nbytes((rows, ns), BF16))
    kern = functools.partial(_proj_kernel, w_src=tuple(w_src), w_transposed=w_transposed,
                             n_blocks=n_blocks, n_p=len(params),
                             tiles=tiles, mode=mode, tm=tm, tn=tw, n_taps=n_taps,
                             row_block=row_block, row_chunk=row_chunk, q_tiles=q_tiles,
                             has_side=side is not None)
    outs = pl.pallas_call(
        kern,
        grid=(n_steps_n, n_steps_m),
        in_specs=in_specs,
        out_specs=out_specs,
        out_shape=out_shape,
        scratch_shapes=scratch,
        compiler_params=_cparams(2, vmem),
        name="proj_" + mode,
    )(*args)
    return outs[0] if side is None else tuple(outs)


def _down_kernel(*refs, has_bias, n_total):
    a_ref, w_ref, x_ref, g_ref = refs[:4]
    b_ref = refs[4] if has_bias else None
    o_ref, xb_ref, rs_ref = refs[-3:]
    n = pl.program_id(1)
    acc = jnp.dot(a_ref[...], w_ref[...], preferred_element_type=F32)
    if has_bias:
        acc = acc + b_ref[...]
    xn = x_ref[...] + acc
    o_ref[...] = xn
    xb_ref[...] = (xn * g_ref[...]).astype(xb_ref.dtype)
    sq = xn * xn
    part = sq[:, 0:LANES]
    for j in range(LANES, sq.shape[1], LANES):
        part = part + sq[:, j:j + LANES]

    @pl.when(n == 0)
    def _():
        rs_ref[...] = part

    @pl.when(n > 0)
    def _():
        rs_ref[...] = rs_ref[...] + part

    @pl.when(n == pl.num_programs(1) - 1)
    def _():
        ms = jnp.sum(rs_ref[...], axis=-1, keepdims=True) / n_total
        rs_ref[...] = jnp.broadcast_to(lax.rsqrt(ms + NORM_EPS), rs_ref.shape)


def down_residual(a, w_bf16, layer, x, gain, bias=None, *, tm=512, tn=512):
    S, K = a.shape
    N = w_bf16.shape[2]
    tm = min(tm, S)
    vmem = 2 * (_nbytes((tm, K), a.dtype) + _nbytes((K, tn), BF16) + 2 * _nbytes((tm, tn), F32)
                + _nbytes((tm, tn), BF16) + _nbytes((tm, LANES), F32))
    vmem += 2 * _nbytes((tm, tn), F32)
    in_specs = [pl.BlockSpec((tm, K), lambda m, n: (m, 0)),
                pl.BlockSpec((None, K, tn), lambda m, n: (layer, 0, n)),
                pl.BlockSpec((tm, tn), lambda m, n: (m, n)),
                pl.BlockSpec((1, tn), lambda m, n: (0, n))]
    args = [a, w_bf16, x, gain.reshape(1, N)]
    if bias is not None:
        in_specs.append(pl.BlockSpec((1, tn), lambda m, n: (0, n)))
        args.append(bias.reshape(1, N))
    x_new, xb, rstd = pl.pallas_call(
        functools.partial(_down_kernel, has_bias=bias is not None, n_total=N),
        grid=(S // tm, N // tn),
        in_specs=in_specs,
        out_specs=[pl.BlockSpec((tm, tn), lambda m, n: (m, n)),
                   pl.BlockSpec((tm, tn), lambda m, n: (m, n)),
                   pl.BlockSpec((tm, LANES), lambda m, n: (m, 0))],
        out_shape=[jax.ShapeDtypeStruct((S, N), F32), jax.ShapeDtypeStruct((S, N), BF16),
                   jax.ShapeDtypeStruct((S, LANES), F32)],
        compiler_params=_cparams(2, vmem),
        name="down_residual",
    )(*args)
    return x_new, (xb, rstd)


def _swa_kernel(sink_ref, q_ref, k_ref, v_ref, cos_ref, sin_ref, o_ref, kprev_ref, vprev_ref,
                *, n_kv, kv_group):
    b = pl.program_id(0)
    D = ATT_HEAD_DIM
    blk = q_ref.shape[0]

    @pl.when(b == 0)
    def _():
        kprev_ref[...] = jnp.zeros(kprev_ref.shape, F32)
        vprev_ref[...] = jnp.zeros(vprev_ref.shape, F32)

    cos = cos_ref[...]
    sin = sin_ref[...]

    def rope(x):
        return x * cos + pltpu.roll(x, D // 2, 1) * sin

    qi = lax.broadcasted_iota(jnp.int32, (blk, 2 * blk), 0)
    kj = lax.broadcasted_iota(jnp.int32, (blk, 2 * blk), 1)
    rel = blk + qi - kj
    valid = (rel >= 0) & (rel < WINDOW) & ((kj >= blk) | (b > 0))
    scale = D ** -0.5

    for g0 in range(0, n_kv, kv_group):
        gs = range(g0, g0 + kv_group)
        kband, vband, s, p = {}, {}, {}, {}
        for g in gs:
            kc = rope(k_ref[:, g * D:(g + 1) * D])
            vc = v_ref[:, g * D:(g + 1) * D]
            kband[g] = jnp.concatenate([kprev_ref[:, g * D:(g + 1) * D], kc], axis=0).astype(BF16)
            vband[g] = jnp.concatenate([vprev_ref[:, g * D:(g + 1) * D], vc], axis=0).astype(BF16)
            kprev_ref[:, g * D:(g + 1) * D] = kc
            vprev_ref[:, g * D:(g + 1) * D] = vc
        hs = [(g, g * ATT_GROUP + gg) for g in gs for gg in range(ATT_GROUP)]
        for g, hh in hs:
            qh = rope(q_ref[:, hh * D:(hh + 1) * D])
            s[hh] = _mm_nt(qh, kband[g]) * scale
        for g, hh in hs:
            sh = jnp.where(valid, s[hh], -jnp.inf)
            sink = sink_ref[hh]
            mx = jnp.maximum(jnp.max(sh, axis=-1, keepdims=True), sink)
            e = jnp.exp(sh - mx)
            denom = jnp.sum(e, axis=-1, keepdims=True) + jnp.exp(sink - mx)
            p[hh] = e / denom
        for g, hh in hs:
            o_ref[:, hh * D:(hh + 1) * D] = _mm(p[hh], vband[g]).astype(o_ref.dtype)


def swa_attention(qkv, sinks, cos_full, sin_signed, n_heads, n_kv):
    S = qkv.shape[0]
    D = ATT_HEAD_DIM
    blk = WINDOW
    qw, kw = n_heads * D, n_kv * D
    vmem = 2 * (_nbytes((blk, qw), F32) + 2 * _nbytes((blk, kw), F32) + 2 * _nbytes((blk, D), F32)
                + _nbytes((blk, qw), BF16)) + 2 * _nbytes((blk, kw), F32)
    return pl.pallas_call(
        functools.partial(_swa_kernel, n_kv=n_kv, kv_group=math.gcd(n_kv, ATT_KV_GROUP)),
        grid=(S // blk,),
        in_specs=[pl.BlockSpec(memory_space=pltpu.SMEM),
                  pl.BlockSpec((blk, qw), lambda b: (b, 0)),
                  pl.BlockSpec((blk, kw), lambda b: (b, qw // kw)),
                  pl.BlockSpec((blk, kw), lambda b: (b, qw // kw + 1)),
                  pl.BlockSpec((blk, D), lambda b: (b, 0)),
                  pl.BlockSpec((blk, D), lambda b: (b, 0))],
        out_specs=pl.BlockSpec((blk, qw), lambda b: (b, 0)),
        out_shape=jax.ShapeDtypeStruct((S, qw), BF16),
        scratch_shapes=[pltpu.VMEM((blk, kw), F32), pltpu.VMEM((blk, kw), F32)],
        compiler_params=_cparams(1, vmem),
        name="swa_attention",
    )(sinks, qkv, qkv, qkv, cos_full, sin_signed)


def _delta_kernel(q_ref, k_ref, v_ref, z_ref, ba_ref, prm_ref, nw_ref, o_ref, st_ref,
                  *, n_v, rep, head_group):
    c = pl.program_id(0)
    C = q_ref.shape[0]
    D = GDN_HEAD_DIM

    @pl.when(c == 0)
    def _():
        st_ref[...] = jnp.zeros(st_ref.shape, F32)

    ba = ba_ref[...]
    beta_t = jax.nn.sigmoid(ba)
    a_sh = ba + prm_ref[1:2, :]
    softplus = jnp.maximum(a_sh, 0.0) + jnp.log(1.0 + jnp.exp(-jnp.abs(a_sh)))
    g_t = -jnp.exp(prm_ref[0:1, :]) * softplus
    row = lax.broadcasted_iota(jnp.int32, (C, LANES), 0)
    gc = g_t
    s = 1
    while s < C:
        gc = gc + jnp.where(row >= s, pltpu.roll(gc, s, 0), 0.0)
        s *= 2
    glast = gc[C - 1:C, :]
    e_gc = jnp.exp(gc)
    e_rest = jnp.exp(glast - gc)
    e_last = jnp.exp(glast)
    gc_t = gc.T

    ii = lax.broadcasted_iota(jnp.int32, (C, C), 0)
    jj = lax.broadcasted_iota(jnp.int32, (C, C), 1)
    lower = ii >= jj
    strict = ii > jj
    same_blk = (ii // GDN_SOLVE_BLOCK) == (jj // GDN_SOLVE_BLOCK)
    nw = nw_ref[...]

    for g0 in range(0, n_v, head_group):
        hs = list(range(g0, g0 + head_group))
        js = sorted({i // rep for i in hs})
        kh = {j: k_ref[:, j * D:(j + 1) * D] for j in js}
        qh = {j: q_ref[:, j * D:(j + 1) * D] for j in js}
        kk = {j: _mm_nt(kh[j], kh[j]) for j in js}
        qk = {j: _mm_nt(qh[j], kh[j]) for j in js}
        st = {i: st_ref[i] for i in hs}
        qg, attn, a_o, tp, pw, rhs = {}, {}, {}, {}, {}, {}
        for i in hs:
            j, gi = i // rep, n_v + i
            bcol = beta_t[:, i:i + 1]
            dec = jnp.exp(jnp.where(lower, gc[:, gi:gi + 1] - gc_t[gi:gi + 1, :], -jnp.inf))
            a_kk = jnp.where(strict, bcol * kk[j] * dec, 0.0)
            attn[i] = qk[j] * dec
            qg[i] = qh[j] * e_gc[:, gi:gi + 1]
            a_d = jnp.where(same_blk, a_kk, 0.0)
            a_o[i] = a_kk - a_d
            tp[i] = -a_d
            pw[i] = tp[i]
            kb = kh[j] * bcol
            rhs[i] = jnp.concatenate([v_ref[:, i * D:(i + 1) * D] * bcol, kb * e_gc[:, gi:gi + 1]],
                                     axis=1)
        n_sq = int(math.log2(GDN_SOLVE_BLOCK)) - 1
        for i in hs:
            pw[i] = _mm(pw[i], pw[i])
        for s in range(n_sq):
            for i in hs:
                tp[i] = tp[i] + pw[i] + _mm(tp[i], pw[i])
            if s + 1 < n_sq:
                for i in hs:
                    pw[i] = _mm(pw[i], pw[i])
        og = {i: _mm(qg[i], st[i]) for i in hs}
        nn = {i: a_o[i] + _mm(tp[i], a_o[i]) for i in hs}
        y = {i: rhs[i] + _mm(tp[i], rhs[i]) for i in hs}
        n2 = {i: _mm(nn[i], nn[i]) for i in hs}
        dd = {i: y[i] - _mm(nn[i], y[i]) for i in hs}
        x = {i: dd[i] + _mm(n2[i], dd[i]) for i in hs}
        v_new = {i: x[i][:, :D] - _mm(x[i][:, D:], st[i]) for i in hs}
        for i in hs:
            j, gi = i // rep, n_v + i
            out = og[i] + _mm(attn[i], v_new[i])
            kd = kh[j] * e_rest[:, gi:gi + 1]
            st_ref[i] = st[i] * e_last[:, gi:gi + 1] + _mm(kd.T, v_new[i])
            ms = jnp.mean(out * out, axis=-1, keepdims=True)
            zh = z_ref[:, i * D:(i + 1) * D]
            o_ref[:, i * D:(i + 1) * D] = (out * lax.rsqrt(ms + NORM_EPS) * nw * _silu(zh)).astype(o_ref.dtype)


def gated_delta(qk, v, z, ba, prm, norm_w, n_qk, n_v):
    S = qk.shape[0]
    D = GDN_HEAD_DIM
    C = GDN_CHUNK
    qw, vw = n_qk * D, n_v * D
    assert GDN_SOLVE_BLOCK * 4 == C
    vmem = 2 * (2 * _nbytes((C, qw), F32) + 2 * _nbytes((C, vw), F32) + _nbytes((C, vw), BF16))
    vmem += _nbytes((n_v, D, D), F32)
    return pl.pallas_call(
        functools.partial(_delta_kernel, n_v=n_v, rep=n_v // n_qk,
                          head_group=math.gcd(n_v, GDN_HEAD_GROUP)),
        grid=(S // C,),
        in_specs=[pl.BlockSpec((C, qw), lambda c: (c, 0)),
                  pl.BlockSpec((C, qw), lambda c: (c, 1)),
                  pl.BlockSpec((C, vw), lambda c: (c, 0)),
                  pl.BlockSpec((C, vw), lambda c: (c, 0)),
                  pl.BlockSpec((C, LANES), lambda c: (c, 0)),
                  pl.BlockSpec((2, LANES), lambda c: (0, 0)),
                  pl.BlockSpec((1, D), lambda c: (0, 0))],
        out_specs=pl.BlockSpec((C, vw), lambda c: (c, 0)),
        out_shape=jax.ShapeDtypeStruct((S, vw), BF16),
        scratch_shapes=[pltpu.VMEM((n_v, D, D), F32)],
        compiler_params=_cparams(1, vmem),
        name="gated_delta",
    )(qk, qk, v, z, ba, prm, norm_w.reshape(1, D))


CONF_HIST = 32


def _conf_kernel(u_ref, w_ref, bdw_ref, g_ref, b_ref, o_ref, ext_ref, sh_ref, y_ref, *, tm, cw,
                 n_c, row_chunk):
    m = pl.program_id(0)
    c = pl.program_id(1)

    @pl.when(m == 0)
    def _():
        ext_ref[c, 0:CONF_HIST, :] = jnp.zeros((CONF_HIST, cw), F32)

    ext_ref[c, CONF_HIST:CONF_HIST + tm, :] = u_ref[...]
    n_rows = CONF_HIST + tm - SUBLANES
    for j in range(1, SUBLANES):
        sh_ref[j - 1, SUBLANES:SUBLANES + n_rows, :] = ext_ref[c, SUBLANES - j:SUBLANES - j + n_rows, :]
    bias = bdw_ref[...]
    for r in range(0, tm, row_chunk):
        y = None
        for k in range(CONF_KERNEL):
            shift = CONF_KERNEL - 1 - k
            a, j = divmod(shift, SUBLANES)
            off = CONF_HIST - SUBLANES * a + r
            if j == 0:
                src = ext_ref[c, off:off + row_chunk, :]
            else:
                src = sh_ref[j - 1, off:off + row_chunk, :]
            term = src * w_ref[k:k + 1, :]
            y = term if y is None else y + term
        y_ref[c, r:r + row_chunk, :] = y + bias
    ext_ref[c, 0:CONF_HIST, :] = ext_ref[c, tm:tm + CONF_HIST, :]

    @pl.when(c == n_c - 1)
    def _():
        d = n_c * cw
        tot = jnp.zeros((tm, 1), F32)
        for j in range(n_c):
            tot = tot + jnp.sum(y_ref[j], axis=-1, keepdims=True)
        mean = tot / d
        var = jnp.zeros((tm, 1), F32)
        for j in range(n_c):
            yc = y_ref[j] - mean
            var = var + jnp.sum(yc * yc, axis=-1, keepdims=True)
        inv = lax.rsqrt(var / d + LN_EPS)
        for j in range(n_c):
            zz = (y_ref[j] - mean) * inv * g_ref[:, j * cw:(j + 1) * cw] + b_ref[:, j * cw:(j + 1) * cw]
            o_ref[:, j * cw:(j + 1) * cw] = _silu(zz).astype(o_ref.dtype)


def conformer_mid(u, w_dw, b_dw, ln_g, ln_b, *, tm=512, cw=512, row_chunk=32):
    S, D = u.shape
    tm, cw = min(tm, S), min(cw, D)
    n_c = D // cw
    vmem = 2 * (_nbytes((tm, cw), F32) + _nbytes((CONF_HIST, cw), F32) + _nbytes((tm, D), BF16))
    vmem += _nbytes((n_c + SUBLANES - 1, tm + CONF_HIST, cw), F32) + 3 * _nbytes((n_c, tm, cw), F32)
    kern = functools.partial(_conf_kernel, tm=tm, cw=cw, n_c=n_c, row_chunk=row_chunk)
    return pl.pallas_call(
        kern,
        grid=(S // tm, n_c),
        in_specs=[pl.BlockSpec((tm, cw), lambda m, c: (m, c)),
                  pl.BlockSpec((CONF_KERNEL, cw), lambda m, c: (0, c)),
                  pl.BlockSpec((1, cw), lambda m, c: (0, c)),
                  pl.BlockSpec((1, D), lambda m, c: (0, 0)),
                  pl.BlockSpec((1, D), lambda m, c: (0, 0))],
        out_specs=pl.BlockSpec((tm, D), lambda m, c: (m, 0)),
        out_shape=jax.ShapeDtypeStruct((S, D), BF16),
        scratch_shapes=[pltpu.VMEM((n_c, tm + CONF_HIST, cw), F32),
                        pltpu.VMEM((SUBLANES - 1, tm + CONF_HIST, cw), F32),
                        pltpu.VMEM((n_c, tm, cw), F32)],
        compiler_params=_cparams(2, vmem),
        name="conformer_mid",
    )(u, w_dw, b_dw.reshape(1, D), ln_g.reshape(1, D), ln_b.reshape(1, D))


def _rope_tables(S):
    half = ATT_HEAD_DIM // 2
    inv = jnp.power(ROPE_THETA, -jnp.arange(half, dtype=F32) / half)
    ang = jnp.arange(S).astype(F32)[:, None] * inv[None, :]
    cos, sin = jnp.cos(ang), jnp.sin(ang)
    return jnp.concatenate([cos, cos], axis=1), jnp.concatenate([-sin, sin], axis=1)


def _wide_tile(n, cap=512):
    t = cap
    while n % t:
        t //= 2
    assert t >= LANES
    return t


PROJ_COLS = 1024


def _out_proj(a, w_o_bf16, x, gain, bias=None):
    return down_residual(a, w_o_bf16[None], 0, x, gain, bias, tm=512,
                         tn=_wide_tile(x.shape[1], PROJ_COLS))


def _mixer_a(hn, x, gain, w_qkv, w_o, sinks, j):
    S, Dm = x.shape
    n_heads = w_o.shape[1] // ATT_HEAD_DIM
    n_kv = n_heads // ATT_GROUP
    n_qkv = w_qkv.shape[2]
    qkv, w_o_bf16 = proj(hn, [(w_qkv, j, 0)], [], "plain", n_qkv, F32,
                         tn=_wide_tile(n_qkv, PROJ_COLS), side=(w_o, j))
    cos_full, sin_signed = _rope_tables(S)
    o = swa_attention(qkv, sinks[j], cos_full, sin_signed, n_heads, n_kv)
    return _out_proj(o, w_o_bf16, x, gain)


def _mixer_b(hn, x, gain, w_in, conv_w, a_log, dt_bias, norm_w, w_o, j):
    S, Dm = x.shape
    D = GDN_HEAD_DIM
    conv_w, a_log, dt_bias, norm_w = conv_w[j], a_log[j], dt_bias[j], norm_w[j]
    n_v = a_log.shape[0]
    val_dim = n_v * D
    key_dim = (conv_w.shape[1] - val_dim) // 2
    n_qk = key_dim // D
    n_qkv = 2 * key_dim + val_dim
    tn = _wide_tile(key_dim, PROJ_COLS)
    n_taps = conv_w.shape[0]
    w_in_t = jnp.swapaxes(w_in, 1, 2)
    qk, w_o_bf16 = proj(hn, [(w_in_t, j, 0)], [(conv_w, 0)], "conv_silu_l2", 2 * key_dim, F32, tn=tn,
                        n_taps=n_taps, q_tiles=key_dim // tn, side=(w_o, j), w_transposed=True)
    v = proj(hn, [(w_in_t, j, 2 * key_dim)], [(conv_w, 2 * key_dim)], "conv_silu", val_dim, F32,
             tn=tn, n_taps=n_taps, w_transposed=True)
    z = proj(hn, [(w_in_t, j, n_qkv)], [], "plain", val_dim, F32, tn=tn, w_transposed=True)
    w_ba_t = jnp.pad(w_in_t[j][n_qkv + val_dim:, :], ((0, LANES - 2 * n_v), (0, 0)))
    ba = proj(hn, [(w_ba_t[None], 0, 0)], [], "plain", LANES, F32, tn=LANES, w_transposed=True)
    prm = jnp.zeros((2, LANES), F32)
    prm = prm.at[0, n_v:2 * n_v].set(a_log).at[1, n_v:2 * n_v].set(dt_bias)
    o = gated_delta(qk, v, z, ba, prm, norm_w, n_qk, n_v)
    return _out_proj(o, w_o_bf16, x, gain)


def _mixer_c(hn, x, gain, w_pw1, b_pw1, w_dw, b_dw, ln_g, ln_b, w_pw2, b_pw2, j):
    S, Dm = x.shape
    b1 = b_pw1[j].reshape(1, -1)
    u, w_pw2_bf16 = proj(hn, [(w_pw1, j, 0), (w_pw1, j, Dm)], [(b1, 0), (b1, Dm)], "glu_bias", Dm,
                         F32, tn=_wide_tile(Dm, PROJ_COLS // 2), side=(w_pw2, j))
    mid = conformer_mid(u, w_dw[j], b_dw[j], ln_g[j], ln_b[j])
    return _out_proj(mid, w_pw2_bf16, x, gain, b_pw2[j])


def _mixer_d(hn, x, gain, w_in, w_conv, w_out, j):
    S, Dm = x.shape
    mid, w_out_bf16 = proj(hn, [(w_in, j, 0), (w_in, j, Dm), (w_in, j, 2 * Dm)], [(w_conv[j], 0)],
                           "short_conv", Dm, BF16, tn=256, n_taps=w_conv.shape[1],
                           side=(w_out, j))
    return _out_proj(mid, w_out_bf16, x, gain)


def _ffn(hn, x, gain, w_gate, w_up, w_conv, b_conv, w_down, i):
    d_ff = w_gate.shape[2]
    mid, w_down_bf16 = proj(hn, [(w_gate, i, 0), (w_up, i, 0)],
                            [(w_conv[i], 0), (b_conv[i].reshape(1, -1), 0)], "conv_glu", d_ff, BF16,
                            tn=256, tiles=PROJ_COLS // 512, n_taps=w_conv.shape[1],
                            side=(w_down, i))
    return down_residual(mid, w_down_bf16[None], 0, x, gain)


def kernel(x, mix_norm, ffn_norm, final_norm, a_w_qkv, a_w_o, a_sinks, b_w_in, b_conv, b_a_log, b_dt_bias, b_norm, b_w_o, c_w_pw1, c_b_pw1, c_w_dw, c_b_dw, c_ln_g, c_ln_b, c_w_pw2, c_b_pw2, d_w_in, d_w_conv, d_w_out, f_w_gate, f_w_up, f_w_conv, f_b_conv, f_w_down):
    B, S, Dm = x.shape
    depth = mix_norm.shape[0]
    outs = []
    for bi in range(B):
        xb = x[bi]
        hn = gain_rstd(xb, mix_norm[0])
        for i in range(depth):
            kind, j = i % 4, i // 4
            g_ffn = ffn_norm[i]
            if kind == 0:
                xb, hn = _mixer_a(hn, xb, g_ffn, a_w_qkv, a_w_o, a_sinks, j)
            elif kind == 1:
                xb, hn = _mixer_b(hn, xb, g_ffn, b_w_in, b_conv, b_a_log, b_dt_bias, b_norm, b_w_o, j)
            elif kind == 2:
                xb, hn = _mixer_c(hn, xb, g_ffn, c_w_pw1, c_b_pw1, c_w_dw, c_b_dw, c_ln_g, c_ln_b,
                                  c_w_pw2, c_b_pw2, j)
            else:
                xb, hn = _mixer_d(hn, xb, g_ffn, d_w_in, d_w_conv, d_w_out, j)
            g_next = mix_norm[i + 1] if i + 1 < depth else final_norm
            xb, hn = _ffn(hn, xb, g_next, f_w_gate, f_w_up, f_w_conv, f_b_conv, f_w_down, i)
        outs.append(rmsnorm(xb, final_norm, F32))
    return jnp.stack(outs, axis=0)
```

```python
import functools
import math

import jax
import jax.numpy as jnp
from jax import lax
from jax.experimental import pallas as pl
from jax.experimental.pallas import tpu as pltpu

F32 = jnp.float32
BF16 = jnp.bfloat16

NORM_EPS = 1e-6
LN_EPS = 1e-5
L2_EPS = 1e-6
ROPE_THETA = 10000.0
ATT_HEAD_DIM = 128
ATT_GROUP = 4
ATT_KV_GROUP = 2
WINDOW = 128
GDN_HEAD_DIM = 128
GDN_CHUNK = 64
GDN_SOLVE_BLOCK = 16
GDN_HEAD_GROUP = 32
CONF_KERNEL = 31

LANES = 128
SUBLANES = 8
VMEM_CAP_BYTES = 61 * 1024 * 1024
VMEM_SLACK_BYTES = 14 * 1024 * 1024


def _cparams(n_axes, vmem_bytes):
    limit = int(min(VMEM_CAP_BYTES, max(vmem_bytes + VMEM_SLACK_BYTES, 16 * 1024 * 1024)))
    return pltpu.CompilerParams(dimension_semantics=("arbitrary",) * n_axes, vmem_limit_bytes=limit)


def _nbytes(shape, dtype):
    return math.prod(shape) * jnp.dtype(dtype).itemsize


def _silu(x):
    return x * jax.nn.sigmoid(x)


def _mm(a, b):
    return jnp.dot(a.astype(BF16), b.astype(BF16), preferred_element_type=F32)


def _mm_nt(a, b):
    return lax.dot_general(a.astype(BF16), b.astype(BF16), (((1,), (1,)), ((), ())),
                           preferred_element_type=F32)


def _rmsnorm_kernel(x_ref, g_ref, o_ref):
    x = x_ref[...]
    ms = jnp.mean(x * x, axis=-1, keepdims=True)
    o_ref[...] = (x * lax.rsqrt(ms + NORM_EPS) * g_ref[...]).astype(o_ref.dtype)


def rmsnorm(x, g, out_dtype, tm=512):
    S, D = x.shape
    tm = min(tm, S)
    vmem = 2 * (_nbytes((tm, D), F32) + _nbytes((tm, D), out_dtype)) + _nbytes((tm, D), F32)
    return pl.pallas_call(
        _rmsnorm_kernel,
        grid=(S // tm,),
        in_specs=[pl.BlockSpec((tm, D), lambda m: (m, 0)),
                  pl.BlockSpec((1, D), lambda m: (0, 0))],
        out_specs=pl.BlockSpec((tm, D), lambda m: (m, 0)),
        out_shape=jax.ShapeDtypeStruct((S, D), out_dtype),
        compiler_params=_cparams(1, vmem),
        name="rmsnorm",
    )(x, g.reshape(1, D))


def _gain_rstd_kernel(x_ref, g_ref, xb_ref, rs_ref):
    x = x_ref[...]
    ms = jnp.mean(x * x, axis=-1, keepdims=True)
    xb_ref[...] = (x * g_ref[...]).astype(xb_ref.dtype)
    rs_ref[...] = jnp.broadcast_to(lax.rsqrt(ms + NORM_EPS), rs_ref.shape)


def gain_rstd(x, g, tm=512):
    S, D = x.shape
    tm = min(tm, S)
    vmem = 2 * (_nbytes((tm, D), F32) + _nbytes((tm, D), BF16)) + _nbytes((tm, D), F32)
    return pl.pallas_call(
        _gain_rstd_kernel,
        grid=(S // tm,),
        in_specs=[pl.BlockSpec((tm, D), lambda m: (m, 0)),
                  pl.BlockSpec((1, D), lambda m: (0, 0))],
        out_specs=[pl.BlockSpec((tm, D), lambda m: (m, 0)),
                   pl.BlockSpec((tm, LANES), lambda m: (m, 0))],
        out_shape=[jax.ShapeDtypeStruct((S, D), BF16), jax.ShapeDtypeStruct((S, LANES), F32)],
        compiler_params=_cparams(1, vmem),
        name="gain_rstd",
    )(x, g.reshape(1, D))


class _TiledParam:
    def __init__(self, refs):
        self.refs = refs

    def __getitem__(self, idx):
        parts = [r[idx] for r in self.refs]
        return parts[0] if len(parts) == 1 else jnp.concatenate(parts, axis=-1)


def _proj_kernel(*refs, w_src, w_transposed, n_blocks, n_p, tiles, mode, tm, tn, n_taps, row_block,
                 row_chunk, q_tiles, has_side):
    n_w = len(w_src)
    h_ref, rs_ref = refs[0], refs[1]
    w_hbm = refs[2:2 + n_w]
    p_all = refs[2 + n_w:2 + n_w + n_p * tiles]
    p_refs = [_TiledParam(p_all[k * tiles:(k + 1) * tiles]) for k in range(n_p)]
    rest = list(refs[2 + n_w + n_p * tiles:])
    side_in = rest.pop(0) if has_side else None
    o_ref = rest.pop(0)
    side_out = rest.pop(0) if has_side else None
    stage_ref, sem, wbf_ref = rest[:3]
    ext_ref = rest[3] if n_taps else None
    tb = tn // tiles
    n = pl.program_id(0)
    m = pl.program_id(1)

    def weight_copies(step):
        copies = []
        for i, (layer, first_block) in enumerate(w_src):
            for t in range(tiles):
                blk = jnp.minimum(step * tiles + t, n_blocks - 1) + first_block
                cols = pl.ds(pl.multiple_of(blk * tb, tb), tb)
                src = w_hbm[i].at[layer, cols, :] if w_transposed else w_hbm[i].at[layer, :, cols]
                k = i * tiles + t
                copies.append(pltpu.make_async_copy(src, stage_ref.at[k], sem.at[k]))
        return copies

    @pl.when(m == 0)
    def _():
        @pl.when(n == 0)
        def _():
            for c in weight_copies(n):
                c.start()

        for k, c in enumerate(weight_copies(n)):
            c.wait()
            if w_transposed:
                wbf_ref[k * tb:(k + 1) * tb, :] = stage_ref[k].astype(BF16)
            else:
                wbf_ref[:, k * tb:(k + 1) * tb] = stage_ref[k].astype(BF16)

        @pl.when(n + 1 < pl.num_programs(0))
        def _():
            for c in weight_copies(n + 1):
                c.start()

        if n_taps:
            ext_ref[0:SUBLANES, :] = jnp.zeros((SUBLANES, tn), F32)

    def run(n_tiles):
        wd = n_tiles * tb

        def mm(h, c0, c1):
            if w_transposed:
                return lax.dot_general(h, wbf_ref[c0:c1, :], (((1,), (1,)), ((), ())),
                                       preferred_element_type=F32)
            return jnp.dot(h, wbf_ref[:, c0:c1], preferred_element_type=F32)

        def dots(r0):
            h = h_ref[r0:r0 + row_block, :]
            rs = rs_ref[r0:r0 + row_block, :]
            rs = jnp.concatenate([rs] * (wd // LANES), axis=1)
            if n_tiles == tiles:
                acc = mm(h, 0, n_w * tn)
                return [acc[:, i * tn:(i + 1) * tn] * rs for i in range(n_w)]
            return [mm(h, i * tn, i * tn + wd) * rs for i in range(n_w)]

        def param(k, idx=slice(None)):
            return p_refs[k][idx, :][:, 0:wd]

        def taps(r):
            y = None
            for k in range(n_taps):
                off = SUBLANES - (n_taps - 1 - k) + r
                term = ext_ref[off:off + row_chunk, 0:wd] * param(0, slice(k, k + 1))
                y = term if y is None else y + term
            return y

        def to_ext(r0, val):
            ext_ref[SUBLANES + r0:SUBLANES + r0 + row_block, 0:wd] = val

        def chunks(r0):
            return [(r, r - r0) for r in range(r0, r0 + row_block, row_chunk)]

        if mode == "plain":
            def matmul(r0):
                return dots(r0)[0]

            def finish(r0, acc):
                o_ref[r0:r0 + row_block, 0:wd] = acc.astype(o_ref.dtype)
        elif mode == "glu_bias":
            def matmul(r0):
                return dots(r0)

            def finish(r0, accs):
                val = accs[0] + param(0)
                gate = accs[1] + param(1)
                o_ref[r0:r0 + row_block, 0:wd] = (val * jax.nn.sigmoid(gate)).astype(o_ref.dtype)
        elif mode == "conv_silu":
            def matmul(r0):
                to_ext(r0, dots(r0)[0])

            def finish(r0, _):
                for r, _ in chunks(r0):
                    o_ref[r:r + row_chunk, 0:wd] = _silu(taps(r)).astype(o_ref.dtype)
        elif mode == "conv_silu_l2":
            scale = jnp.where(n < q_tiles, GDN_HEAD_DIM ** -0.5, 1.0).astype(F32)

            def matmul(r0):
                to_ext(r0, dots(r0)[0])

            def finish(r0, _):
                for r, _ in chunks(r0):
                    y = _silu(taps(r))
                    for j in range(0, wd, GDN_HEAD_DIM):
                        yh = y[:, j:j + GDN_HEAD_DIM]
                        inv = lax.rsqrt(jnp.sum(yh * yh, axis=-1, keepdims=True) + L2_EPS)
                        o_ref[r:r + row_chunk, j:j + GDN_HEAD_DIM] = (yh * inv * scale).astype(o_ref.dtype)
        elif mode == "short_conv":
            def matmul(r0):
                bg, cg, xin = dots(r0)
                to_ext(r0, cg * xin)
                return bg

            def finish(r0, bg):
                for r, rr in chunks(r0):
                    o_ref[r:r + row_chunk, 0:wd] = (bg[rr:rr + row_chunk, :] * taps(r)).astype(o_ref.dtype)
        elif mode == "conv_glu":
            def matmul(r0):
                gate, up = dots(r0)
                to_ext(r0, gate)
                return up

            def finish(r0, up):
                bias = param(1)
                for r, rr in chunks(r0):
                    o_ref[r:r + row_chunk, 0:wd] = (_silu(taps(r) + bias) * up[rr:rr + row_chunk, :]).astype(o_ref.dtype)
        else:
            raise ValueError(mode)

        pending = None
        for r0 in range(0, tm, row_block):
            acc = matmul(r0)
            if pending is not None:
                finish(*pending)
            pending = (r0, acc)
        finish(*pending)
        if n_taps:
            ext_ref[0:SUBLANES, 0:wd] = ext_ref[tm:tm + SUBLANES, 0:wd]

    if n_blocks % tiles == 0:
        run(tiles)
    else:
        last = pl.num_programs(0) - 1

        @pl.when(n < last)
        def _():
            run(tiles)

        @pl.when(n == last)
        def _():
            run(n_blocks % tiles)

    if has_side:
        side_out[...] = side_in[...].astype(side_out.dtype)


def _side_rows(n_rows, n_steps):
    bf16_rows = 2 * SUBLANES
    for r in range(bf16_rows, n_rows + 1, bf16_rows):
        if n_rows % r == 0 and n_rows // r <= n_steps:
            return r
    raise ValueError((n_rows, n_steps))


def proj(hn, weights, params, mode, n_out, out_dtype, *, tn, tiles=1, tm=None, n_taps=0, q_tiles=0,
         row_block=512, row_chunk=128, side=None, w_transposed=False):
    h, rstd = hn
    S, K = h.shape
    if tm is None:
        tm = 1024 if jnp.dtype(out_dtype).itemsize == 2 else 512
    tm = min(tm, S)
    row_block = min(row_block, tm)
    assert S % tm == 0 and n_out % tn == 0 and tm % row_block == 0 and row_block % row_chunk == 0
    n_w = len(weights)
    n_blocks = n_out // tn
    in_specs = [pl.BlockSpec((tm, K), lambda n, m: (m, 0)),
                pl.BlockSpec((tm, LANES), lambda n, m: (m, 0))]
    args = [h, rstd]
    vmem = 2 * (_nbytes((tm, K), h.dtype) + _nbytes((tm, LANES), F32))

    def col_block(n, t, o):
        return jnp.minimum(n * tiles + t, n_blocks - 1) + o

    w_src = []
    for w, layer, off in weights:
        assert w.shape[2 if w_transposed else 1] == K and off % tn == 0 and w.dtype == F32
        in_specs.append(pl.BlockSpec(memory_space=pl.ANY))
        args.append(w)
        w_src.append((layer, off // tn))
        vmem += tiles * (_nbytes((K, tn), F32) + _nbytes((K, tn), BF16))
    for p, off in params:
        for t in range(tiles):
            in_specs.append(pl.BlockSpec(
                (p.shape[0], tn), lambda n, m, t=t, o=off // tn: (0, col_block(n, t, o))))
            args.append(p)
        vmem += tiles * 2 * _nbytes((SUBLANES, tn), F32)
    tw = tiles * tn
    scratch = [pltpu.VMEM((n_w * tiles, tn, K) if w_transposed else (n_w * tiles, K, tn), F32),
               pltpu.SemaphoreType.DMA((n_w * tiles,)),
               pltpu.VMEM((n_w * tw, K) if w_transposed else (K, n_w * tw), BF16)]
    if n_taps:
        scratch.append(pltpu.VMEM((tm + SUBLANES, tw), F32))
        vmem += _nbytes((tm + SUBLANES, tw), F32)
    vmem += 2 * _nbytes((tm, tw), out_dtype) + n_w * _nbytes((row_block, tw), F32)
    n_steps_n, n_steps_m = pl.cdiv(n_blocks, tiles), S // tm
    out_specs = [pl.BlockSpec((tm, tw), lambda n, m: (m, n))]
    out_shape = [jax.ShapeDtypeStruct((S, n_out), out_dtype)]
    if side is not None:
        w_side, side_layer = side
        _, ks, ns = w_side.shape
        rows = _side_rows(ks, n_steps_n * n_steps_m)

        def side_block(n, m):
            return jnp.minimum(n * n_steps_m + m, ks // rows - 1)

        in_specs.append(pl.BlockSpec((None, rows, ns), lambda n, m: (side_layer, side_block(n, m), 0)))
        args.append(w_side)
        out_specs.append(pl.BlockSpec((rows, ns), lambda n, m: (side_block(n, m), 0)))
        out_shape.append(jax.ShapeDtypeStruct((ks, ns), BF16))
        vmem += 2 * (_nbytes((rows, ns), F32) + _nbytes((rows, ns), BF16))
    kern = functools.partial(_proj_kernel, w_src=tuple(w_src), w_transposed=w_transposed,
                             n_blocks=n_blocks, n_p=len(params),
                             tiles=tiles, mode=mode, tm=tm, tn=tw, n_taps=n_taps,
                             row_block=row_block, row_chunk=row_chunk, q_tiles=q_tiles,
                             has_side=side is not None)
    outs = pl.pallas_call(
        kern,
        grid=(n_steps_n, n_steps_m),
        in_specs=in_specs,
        out_specs=out_specs,
        out_shape=out_shape,
        scratch_shapes=scratch,
        compiler_params=_cparams(2, vmem),
        name="proj_" + mode,
    )(*args)
    return outs[0] if side is None else tuple(outs)


def _down_kernel(*refs, has_bias, n_total, row_block):
    a_ref, w_ref, x_ref, g_ref = refs[:4]
    b_ref = refs[4] if has_bias else None
    o_ref, xb_ref, rs_ref = refs[-3:]
    n = pl.program_id(1)
    tm = a_ref.shape[0]

    @pl.when(n == 0)
    def _():
        rs_ref[...] = jnp.zeros(rs_ref.shape, F32)

    def matmul(r0):
        return jnp.dot(a_ref[r0:r0 + row_block, :], w_ref[...], preferred_element_type=F32)

    def finish(r0, acc):
        rows = slice(r0, r0 + row_block)
        if has_bias:
            acc = acc + b_ref[...]
        xn = x_ref[rows, :] + acc
        o_ref[rows, :] = xn
        xb_ref[rows, :] = (xn * g_ref[...]).astype(xb_ref.dtype)
        sq = xn * xn
        part = sq[:, 0:LANES]
        for j in range(LANES, sq.shape[1], LANES):
            part = part + sq[:, j:j + LANES]
        rs_ref[rows, :] = rs_ref[rows, :] + part

    pending = None
    for r0 in range(0, tm, row_block):
        acc = matmul(r0)
        if pending is not None:
            finish(*pending)
        pending = (r0, acc)
    finish(*pending)

    @pl.when(n == pl.num_programs(1) - 1)
    def _():
        ms = jnp.sum(rs_ref[...], axis=-1, keepdims=True) / n_total
        rs_ref[...] = jnp.broadcast_to(lax.rsqrt(ms + NORM_EPS), rs_ref.shape)


def down_residual(a, w_bf16, layer, x, gain, bias=None, *, tm=512, tn=512, row_blocks=1):
    S, K = a.shape
    N = w_bf16.shape[2]
    tm = min(tm, S)
    row_block = tm // row_blocks
    assert row_block % (2 * SUBLANES) == 0
    vmem = 2 * (_nbytes((tm, K), a.dtype) + _nbytes((K, tn), BF16) + 2 * _nbytes((tm, tn), F32)
                + _nbytes((tm, tn), BF16) + _nbytes((tm, LANES), F32))
    vmem += 2 * _nbytes((tm, tn), F32)
    in_specs = [pl.BlockSpec((tm, K), lambda m, n: (m, 0)),
                pl.BlockSpec((None, K, tn), lambda m, n: (layer, 0, n)),
                pl.BlockSpec((tm, tn), lambda m, n: (m, n)),
                pl.BlockSpec((1, tn), lambda m, n: (0, n))]
    args = [a, w_bf16, x, gain.reshape(1, N)]
    if bias is not None:
        in_specs.append(pl.BlockSpec((1, tn), lambda m, n: (0, n)))
        args.append(bias.reshape(1, N))
    x_new, xb, rstd = pl.pallas_call(
        functools.partial(_down_kernel, has_bias=bias is not None, n_total=N, row_block=row_block),
        grid=(S // tm, N // tn),
        in_specs=in_specs,
        out_specs=[pl.BlockSpec((tm, tn), lambda m, n: (m, n)),
                   pl.BlockSpec((tm, tn), lambda m, n: (m, n)),
                   pl.BlockSpec((tm, LANES), lambda m, n: (m, 0))],
        out_shape=[jax.ShapeDtypeStruct((S, N), F32), jax.ShapeDtypeStruct((S, N), BF16),
                   jax.ShapeDtypeStruct((S, LANES), F32)],
        compiler_params=_cparams(2, vmem),
        name="down_residual",
    )(*args)
    return x_new, (xb, rstd)


def _swa_kernel(sink_ref, q_ref, k_ref, v_ref, cos_ref, sin_ref, o_ref, kprev_ref, vprev_ref,
                *, n_kv, kv_group):
    b = pl.program_id(0)
    D = ATT_HEAD_DIM
    blk = q_ref.shape[0]

    @pl.when(b == 0)
    def _():
        kprev_ref[...] = jnp.zeros(kprev_ref.shape, F32)
        vprev_ref[...] = jnp.zeros(vprev_ref.shape, F32)

    cos = cos_ref[...]
    sin = sin_ref[...]

    def rope(x):
        return x * cos + pltpu.roll(x, D // 2, 1) * sin

    qi = lax.broadcasted_iota(jnp.int32, (blk, 2 * blk), 0)
    kj = lax.broadcasted_iota(jnp.int32, (blk, 2 * blk), 1)
    rel = blk + qi - kj
    valid = (rel >= 0) & (rel < WINDOW) & ((kj >= blk) | (b > 0))
    scale = D ** -0.5

    for g0 in range(0, n_kv, kv_group):
        gs = range(g0, g0 + kv_group)
        kband, vband, s, p = {}, {}, {}, {}
        for g in gs:
            kc = rope(k_ref[:, g * D:(g + 1) * D])
            vc = v_ref[:, g * D:(g + 1) * D]
            kband[g] = jnp.concatenate([kprev_ref[:, g * D:(g + 1) * D], kc], axis=0).astype(BF16)
            vband[g] = jnp.concatenate([vprev_ref[:, g * D:(g + 1) * D], vc], axis=0).astype(BF16)
            kprev_ref[:, g * D:(g + 1) * D] = kc
            vprev_ref[:, g * D:(g + 1) * D] = vc
        hs = [(g, g * ATT_GROUP + gg) for g in gs for gg in range(ATT_GROUP)]
        for g, hh in hs:
            qh = rope(q_ref[:, hh * D:(hh + 1) * D])
            s[hh] = _mm_nt(qh, kband[g]) * scale
        for g, hh in hs:
            sh = jnp.where(valid, s[hh], -jnp.inf)
            sink = sink_ref[hh]
            mx = jnp.maximum(jnp.max(sh, axis=-1, keepdims=True), sink)
            e = jnp.exp(sh - mx)
            denom = jnp.sum(e, axis=-1, keepdims=True) + jnp.exp(sink - mx)
            p[hh] = e / denom
        for g, hh in hs:
            o_ref[:, hh * D:(hh + 1) * D] = _mm(p[hh], vband[g]).astype(o_ref.dtype)


def swa_attention(qkv, sinks, cos_full, sin_signed, n_heads, n_kv):
    S = qkv.shape[0]
    D = ATT_HEAD_DIM
    blk = WINDOW
    qw, kw = n_heads * D, n_kv * D
    vmem = 2 * (_nbytes((blk, qw), F32) + 2 * _nbytes((blk, kw), F32) + 2 * _nbytes((blk, D), F32)
                + _nbytes((blk, qw), BF16)) + 2 * _nbytes((blk, kw), F32)
    return pl.pallas_call(
        functools.partial(_swa_kernel, n_kv=n_kv, kv_group=math.gcd(n_kv, ATT_KV_GROUP)),
        grid=(S // blk,),
        in_specs=[pl.BlockSpec(memory_space=pltpu.SMEM),
                  pl.BlockSpec((blk, qw), lambda b: (b, 0)),
                  pl.BlockSpec((blk, kw), lambda b: (b, qw // kw)),
                  pl.BlockSpec((blk, kw), lambda b: (b, qw // kw + 1)),
                  pl.BlockSpec((blk, D), lambda b: (b, 0)),
                  pl.BlockSpec((blk, D), lambda b: (b, 0))],
        out_specs=pl.BlockSpec((blk, qw), lambda b: (b, 0)),
        out_shape=jax.ShapeDtypeStruct((S, qw), BF16),
        scratch_shapes=[pltpu.VMEM((blk, kw), F32), pltpu.VMEM((blk, kw), F32)],
        compiler_params=_cparams(1, vmem),
        name="swa_attention",
    )(sinks, qkv, qkv, qkv, cos_full, sin_signed)


def _delta_kernel(q_ref, k_ref, v_ref, z_ref, ba_ref, prm_ref, nw_ref, o_ref, st_ref,
                  *, n_v, rep, head_group):
    c = pl.program_id(0)
    C = q_ref.shape[0]
    D = GDN_HEAD_DIM

    @pl.when(c == 0)
    def _():
        st_ref[...] = jnp.zeros(st_ref.shape, F32)

    ba = ba_ref[...]
    beta_t = jax.nn.sigmoid(ba)
    a_sh = ba + prm_ref[1:2, :]
    softplus = jnp.maximum(a_sh, 0.0) + jnp.log(1.0 + jnp.exp(-jnp.abs(a_sh)))
    g_t = -jnp.exp(prm_ref[0:1, :]) * softplus
    row = lax.broadcasted_iota(jnp.int32, (C, LANES), 0)
    gc = g_t
    s = 1
    while s < C:
        gc = gc + jnp.where(row >= s, pltpu.roll(gc, s, 0), 0.0)
        s *= 2
    glast = gc[C - 1:C, :]
    e_gc = jnp.exp(gc)
    e_rest = jnp.exp(glast - gc)
    e_last = jnp.exp(glast)
    gc_t = gc.T

    ii = lax.broadcasted_iota(jnp.int32, (C, C), 0)
    jj = lax.broadcasted_iota(jnp.int32, (C, C), 1)
    lower = ii >= jj
    strict = ii > jj
    same_blk = (ii // GDN_SOLVE_BLOCK) == (jj // GDN_SOLVE_BLOCK)
    nw = nw_ref[...]

    for g0 in range(0, n_v, head_group):
        hs = list(range(g0, g0 + head_group))
        js = sorted({i // rep for i in hs})
        kh = {j: k_ref[:, j * D:(j + 1) * D] for j in js}
        qh = {j: q_ref[:, j * D:(j + 1) * D] for j in js}
        kk = {j: _mm_nt(kh[j], kh[j]) for j in js}
        qk = {j: _mm_nt(qh[j], kh[j]) for j in js}
        st = {i: st_ref[i] for i in hs}
        qg, attn, a_o, tp, pw, rhs = {}, {}, {}, {}, {}, {}
        for i in hs:
            j, gi = i // rep, n_v + i
            bcol = beta_t[:, i:i + 1]
            dec = jnp.exp(jnp.where(lower, gc[:, gi:gi + 1] - gc_t[gi:gi + 1, :], -jnp.inf))
            a_kk = jnp.where(strict, bcol * kk[j] * dec, 0.0)
            attn[i] = qk[j] * dec
            qg[i] = qh[j] * e_gc[:, gi:gi + 1]
            a_d = jnp.where(same_blk, a_kk, 0.0)
            a_o[i] = a_kk - a_d
            tp[i] = -a_d
            pw[i] = tp[i]
            kb = kh[j] * bcol
            rhs[i] = jnp.concatenate([v_ref[:, i * D:(i + 1) * D] * bcol, kb * e_gc[:, gi:gi + 1]],
                                     axis=1)
        n_sq = int(math.log2(GDN_SOLVE_BLOCK)) - 1
        for i in hs:
            pw[i] = _mm(pw[i], pw[i])
        for s in range(n_sq):
            for i in hs:
                tp[i] = tp[i] + pw[i] + _mm(tp[i], pw[i])
            if s + 1 < n_sq:
                for i in hs:
                    pw[i] = _mm(pw[i], pw[i])
        og = {i: _mm(qg[i], st[i]) for i in hs}
        nn = {i: a_o[i] + _mm(tp[i], a_o[i]) for i in hs}
        y = {i: rhs[i] + _mm(tp[i], rhs[i]) for i in hs}
        n2 = {i: _mm(nn[i], nn[i]) for i in hs}
        dd = {i: y[i] - _mm(nn[i], y[i]) for i in hs}
        x = {i: dd[i] + _mm(n2[i], dd[i]) for i in hs}
        v_new = {i: x[i][:, :D] - _mm(x[i][:, D:], st[i]) for i in hs}
        for i in hs:
            j, gi = i // rep, n_v + i
            out = og[i] + _mm(attn[i], v_new[i])
            kd = kh[j] * e_rest[:, gi:gi + 1]
            st_ref[i] = st[i] * e_last[:, gi:gi + 1] + _mm(kd.T, v_new[i])
            ms = jnp.mean(out * out, axis=-1, keepdims=True)
            zh = z_ref[:, i * D:(i + 1) * D]
            o_ref[:, i * D:(i + 1) * D] = (out * lax.rsqrt(ms + NORM_EPS) * nw * _silu(zh)).astype(o_ref.dtype)


def gated_delta(qk, v, z, ba, prm, norm_w, n_qk, n_v):
    S = qk.shape[0]
    D = GDN_HEAD_DIM
    C = GDN_CHUNK
    qw, vw = n_qk * D, n_v * D
    assert GDN_SOLVE_BLOCK * 4 == C
    vmem = 2 * (2 * _nbytes((C, qw), F32) + 2 * _nbytes((C, vw), F32) + _nbytes((C, vw), BF16))
    vmem += _nbytes((n_v, D, D), F32)
    return pl.pallas_call(
        functools.partial(_delta_kernel, n_v=n_v, rep=n_v // n_qk,
                          head_group=math.gcd(n_v, GDN_HEAD_GROUP)),
        grid=(S // C,),
        in_specs=[pl.BlockSpec((C, qw), lambda c: (c, 0)),
                  pl.BlockSpec((C, qw), lambda c: (c, 1)),
                  pl.BlockSpec((C, vw), lambda c: (c, 0)),
                  pl.BlockSpec((C, vw), lambda c: (c, 0)),
                  pl.BlockSpec((C, LANES), lambda c: (c, 0)),
                  pl.BlockSpec((2, LANES), lambda c: (0, 0)),
                  pl.BlockSpec((1, D), lambda c: (0, 0))],
        out_specs=pl.BlockSpec((C, vw), lambda c: (c, 0)),
        out_shape=jax.ShapeDtypeStruct((S, vw), BF16),
        scratch_shapes=[pltpu.VMEM((n_v, D, D), F32)],
        compiler_params=_cparams(1, vmem),
        name="gated_delta",
    )(qk, qk, v, z, ba, prm, norm_w.reshape(1, D))


CONF_HIST = 32


def _conf_kernel(u_ref, w_ref, bdw_ref, g_ref, b_ref, o_ref, ext_ref, sh_ref, y_ref, *, tm, cw,
                 n_c, row_chunk):
    m = pl.program_id(0)
    c = pl.program_id(1)

    @pl.when(m == 0)
    def _():
        ext_ref[c, 0:CONF_HIST, :] = jnp.zeros((CONF_HIST, cw), F32)

    ext_ref[c, CONF_HIST:CONF_HIST + tm, :] = u_ref[...]
    n_rows = CONF_HIST + tm - SUBLANES
    for j in range(1, SUBLANES):
        sh_ref[j - 1, SUBLANES:SUBLANES + n_rows, :] = ext_ref[c, SUBLANES - j:SUBLANES - j + n_rows, :]
    bias = bdw_ref[...]
    for r in range(0, tm, row_chunk):
        y = None
        for k in range(CONF_KERNEL):
            shift = CONF_KERNEL - 1 - k
            a, j = divmod(shift, SUBLANES)
            off = CONF_HIST - SUBLANES * a + r
            if j == 0:
                src = ext_ref[c, off:off + row_chunk, :]
            else:
                src = sh_ref[j - 1, off:off + row_chunk, :]
            term = src * w_ref[k:k + 1, :]
            y = term if y is None else y + term
        y_ref[c, r:r + row_chunk, :] = y + bias
    ext_ref[c, 0:CONF_HIST, :] = ext_ref[c, tm:tm + CONF_HIST, :]

    @pl.when(c == n_c - 1)
    def _():
        d = n_c * cw
        tot = jnp.zeros((tm, 1), F32)
        for j in range(n_c):
            tot = tot + jnp.sum(y_ref[j], axis=-1, keepdims=True)
        mean = tot / d
        var = jnp.zeros((tm, 1), F32)
        for j in range(n_c):
            yc = y_ref[j] - mean
            var = var + jnp.sum(yc * yc, axis=-1, keepdims=True)
        inv = lax.rsqrt(var / d + LN_EPS)
        for j in range(n_c):
            zz = (y_ref[j] - mean) * inv * g_ref[:, j * cw:(j + 1) * cw] + b_ref[:, j * cw:(j + 1) * cw]
            o_ref[:, j * cw:(j + 1) * cw] = _silu(zz).astype(o_ref.dtype)


def conformer_mid(u, w_dw, b_dw, ln_g, ln_b, *, tm=512, cw=512, row_chunk=32):
    S, D = u.shape
    tm, cw = min(tm, S), min(cw, D)
    n_c = D // cw
    vmem = 2 * (_nbytes((tm, cw), F32) + _nbytes((CONF_HIST, cw), F32) + _nbytes((tm, D), BF16))
    vmem += _nbytes((n_c + SUBLANES - 1, tm + CONF_HIST, cw), F32) + 3 * _nbytes((n_c, tm, cw), F32)
    kern = functools.partial(_conf_kernel, tm=tm, cw=cw, n_c=n_c, row_chunk=row_chunk)
    return pl.pallas_call(
        kern,
        grid=(S // tm, n_c),
        in_specs=[pl.BlockSpec((tm, cw), lambda m, c: (m, c)),
                  pl.BlockSpec((CONF_KERNEL, cw), lambda m, c: (0, c)),
                  pl.BlockSpec((1, cw), lambda m, c: (0, c)),
                  pl.BlockSpec((1, D), lambda m, c: (0, 0)),
                  pl.BlockSpec((1, D), lambda m, c: (0, 0))],
        out_specs=pl.BlockSpec((tm, D), lambda m, c: (m, 0)),
        out_shape=jax.ShapeDtypeStruct((S, D), BF16),
        scratch_shapes=[pltpu.VMEM((n_c, tm + CONF_HIST, cw), F32),
                        pltpu.VMEM((SUBLANES - 1, tm + CONF_HIST, cw), F32),
                        pltpu.VMEM((n_c, tm, cw), F32)],
        compiler_params=_cparams(2, vmem),
        name="conformer_mid",
    )(u, w_dw, b_dw.reshape(1, D), ln_g.reshape(1, D), ln_b.reshape(1, D))


def _rope_tables(S):
    half = ATT_HEAD_DIM // 2
    inv = jnp.power(ROPE_THETA, -jnp.arange(half, dtype=F32) / half)
    ang = jnp.arange(S).astype(F32)[:, None] * inv[None, :]
    cos, sin = jnp.cos(ang), jnp.sin(ang)
    return jnp.concatenate([cos, cos], axis=1), jnp.concatenate([-sin, sin], axis=1)


def _wide_tile(n, cap=512):
    t = cap
    while n % t:
        t //= 2
    assert t >= LANES
    return t


PROJ_COLS = 1024


def _out_proj(a, w_o_bf16, x, gain, bias=None):
    return down_residual(a, w_o_bf16[None], 0, x, gain, bias, tm=1024, tn=_wide_tile(x.shape[1]))


def _mixer_a(hn, x, gain, w_qkv, w_o, sinks, j):
    S, Dm = x.shape
    n_heads = w_o.shape[1] // ATT_HEAD_DIM
    n_kv = n_heads // ATT_GROUP
    n_qkv = w_qkv.shape[2]
    qkv, w_o_bf16 = proj(hn, [(w_qkv, j, 0)], [], "plain", n_qkv, F32,
                         tn=_wide_tile(n_qkv, PROJ_COLS), side=(w_o, j))
    cos_full, sin_signed = _rope_tables(S)
    o = swa_attention(qkv, sinks[j], cos_full, sin_signed, n_heads, n_kv)
    return _out_proj(o, w_o_bf16, x, gain)


def _mixer_b(hn, x, gain, w_in, conv_w, a_log, dt_bias, norm_w, w_o, j):
    S, Dm = x.shape
    D = GDN_HEAD_DIM
    conv_w, a_log, dt_bias, norm_w = conv_w[j], a_log[j], dt_bias[j], norm_w[j]
    n_v = a_log.shape[0]
    val_dim = n_v * D
    key_dim = (conv_w.shape[1] - val_dim) // 2
    n_qk = key_dim // D
    n_qkv = 2 * key_dim + val_dim
    tn = _wide_tile(key_dim, PROJ_COLS)
    n_taps = conv_w.shape[0]
    w_in_t = jnp.swapaxes(w_in, 1, 2)
    qk, w_o_bf16 = proj(hn, [(w_in_t, j, 0)], [(conv_w, 0)], "conv_silu_l2", 2 * key_dim, F32, tn=tn,
                        n_taps=n_taps, q_tiles=key_dim // tn, side=(w_o, j), w_transposed=True)
    v = proj(hn, [(w_in_t, j, 2 * key_dim)], [(conv_w, 2 * key_dim)], "conv_silu", val_dim, F32,
             tn=tn, n_taps=n_taps, w_transposed=True)
    z = proj(hn, [(w_in_t, j, n_qkv)], [], "plain", val_dim, F32, tn=tn, w_transposed=True)
    w_ba_t = jnp.pad(w_in_t[j][n_qkv + val_dim:, :], ((0, LANES - 2 * n_v), (0, 0)))
    ba = proj(hn, [(w_ba_t[None], 0, 0)], [], "plain", LANES, F32, tn=LANES, w_transposed=True)
    prm = jnp.zeros((2, LANES), F32)
    prm = prm.at[0, n_v:2 * n_v].set(a_log).at[1, n_v:2 * n_v].set(dt_bias)
    o = gated_delta(qk, v, z, ba, prm, norm_w, n_qk, n_v)
    return _out_proj(o, w_o_bf16, x, gain)


def _mixer_c(hn, x, gain, w_pw1, b_pw1, w_dw, b_dw, ln_g, ln_b, w_pw2, b_pw2, j):
    S, Dm = x.shape
    b1 = b_pw1[j].reshape(1, -1)
    u, w_pw2_bf16 = proj(hn, [(w_pw1, j, 0), (w_pw1, j, Dm)], [(b1, 0), (b1, Dm)], "glu_bias", Dm,
                         F32, tn=_wide_tile(Dm, PROJ_COLS // 2), side=(w_pw2, j))
    mid = conformer_mid(u, w_dw[j], b_dw[j], ln_g[j], ln_b[j])
    return _out_proj(mid, w_pw2_bf16, x, gain, b_pw2[j])


def _mixer_d(hn, x, gain, w_in, w_conv, w_out, j):
    S, Dm = x.shape
    mid, w_out_bf16 = proj(hn, [(w_in, j, 0), (w_in, j, Dm), (w_in, j, 2 * Dm)], [(w_conv[j], 0)],
                           "short_conv", Dm, BF16, tn=256, n_taps=w_conv.shape[1],
                           side=(w_out, j))
    return _out_proj(mid, w_out_bf16, x, gain)


def _ffn(hn, x, gain, w_gate, w_up, w_conv, b_conv, w_down, i):
    d_ff = w_gate.shape[2]
    mid, w_down_bf16 = proj(hn, [(w_gate, i, 0), (w_up, i, 0)],
                            [(w_conv[i], 0), (b_conv[i].reshape(1, -1), 0)], "conv_glu", d_ff, BF16,
                            tn=256, tiles=PROJ_COLS // 512, n_taps=w_conv.shape[1],
                            side=(w_down, i))
    return down_residual(mid, w_down_bf16[None], 0, x, gain)


def kernel(x, mix_norm, ffn_norm, final_norm, a_w_qkv, a_w_o, a_sinks, b_w_in, b_conv, b_a_log, b_dt_bias, b_norm, b_w_o, c_w_pw1, c_b_pw1, c_w_dw, c_b_dw, c_ln_g, c_ln_b, c_w_pw2, c_b_pw2, d_w_in, d_w_conv, d_w_out, f_w_gate, f_w_up, f_w_conv, f_b_conv, f_w_down):
    B, S, Dm = x.shape
    depth = mix_norm.shape[0]
    outs = []
    for bi in range(B):
        xb = x[bi]
        hn = gain_rstd(xb, mix_norm[0])
        for i in range(depth):
            kind, j = i % 4, i // 4
            g_ffn = ffn_norm[i]
            if kind == 0:
                xb, hn = _mixer_a(hn, xb, g_ffn, a_w_qkv, a_w_o, a_sinks, j)
            elif kind == 1:
                xb, hn = _mixer_b(hn, xb, g_ffn, b_w_in, b_conv, b_a_log, b_dt_bias, b_norm, b_w_o, j)
            elif kind == 2:
                xb, hn = _mixer_c(hn, xb, g_ffn, c_w_pw1, c_b_pw1, c_w_dw, c_b_dw, c_ln_g, c_ln_b,
                                  c_w_pw2, c_b_pw2, j)
            else:
                xb, hn = _mixer_d(hn, xb, g_ffn, d_w_in, d_w_conv, d_w_out, j)
            g_next = mix_norm[i + 1] if i + 1 < depth else final_norm
            xb, hn = _ffn(hn, xb, g_next, f_w_gate, f_w_up, f_w_conv, f_b_conv, f_w_down, i)
        outs.append(rmsnorm(xb, final_norm, F32))
    return jnp.stack(outs, axis=0)
```

```python
import functools
import math

import jax
import jax.numpy as jnp
from jax import lax
from jax.experimental import pallas as pl
from jax.experimental.pallas import tpu as pltpu

F32 = jnp.float32
BF16 = jnp.bfloat16

NORM_EPS = 1e-6
LN_EPS = 1e-5
L2_EPS = 1e-6
ROPE_THETA = 10000.0
ATT_HEAD_DIM = 128
ATT_GROUP = 4
ATT_KV_GROUP = 2
WINDOW = 128
GDN_HEAD_DIM = 128
GDN_CHUNK = 64
GDN_SOLVE_BLOCK = 16
GDN_HEAD_GROUP = 32
CONF_KERNEL = 31

LANES = 128
SUBLANES = 8
MXU_COLS = 256
PROJ_COLS = 1024
PROJ_ROWS_BF16_OUT = 1024
PROJ_ROWS_F32_OUT = 512
VMEM_CAP_BYTES = 61 * 1024 * 1024
VMEM_SLACK_BYTES = 14 * 1024 * 1024


def _cparams(n_axes, vmem_bytes):
    limit = int(min(VMEM_CAP_BYTES, max(vmem_bytes + VMEM_SLACK_BYTES, 16 * 1024 * 1024)))
    return pltpu.CompilerParams(dimension_semantics=("arbitrary",) * n_axes, vmem_limit_bytes=limit)


def _nbytes(shape, dtype):
    return math.prod(shape) * jnp.dtype(dtype).itemsize


def _silu(x):
    return x * jax.nn.sigmoid(x)


def _mm(a, b):
    return jnp.dot(a.astype(BF16), b.astype(BF16), preferred_element_type=F32)


def _mm_nt(a, b):
    return lax.dot_general(a.astype(BF16), b.astype(BF16), (((1,), (1,)), ((), ())),
                           preferred_element_type=F32)


def _rmsnorm_kernel(x_ref, g_ref, o_ref):
    x = x_ref[...]
    ms = jnp.mean(x * x, axis=-1, keepdims=True)
    o_ref[...] = (x * lax.rsqrt(ms + NORM_EPS) * g_ref[...]).astype(o_ref.dtype)


def rmsnorm(x, g, out_dtype, tm=512):
    S, D = x.shape
    tm = min(tm, S)
    vmem = 2 * (_nbytes((tm, D), F32) + _nbytes((tm, D), out_dtype)) + _nbytes((tm, D), F32)
    return pl.pallas_call(
        _rmsnorm_kernel,
        grid=(S // tm,),
        in_specs=[pl.BlockSpec((tm, D), lambda m: (m, 0)),
                  pl.BlockSpec((1, D), lambda m: (0, 0))],
        out_specs=pl.BlockSpec((tm, D), lambda m: (m, 0)),
        out_shape=jax.ShapeDtypeStruct((S, D), out_dtype),
        compiler_params=_cparams(1, vmem),
        name="rmsnorm",
    )(x, g.reshape(1, D))


def _gain_rstd_kernel(x_ref, g_ref, xb_ref, rs_ref):
    x = x_ref[...]
    ms = jnp.mean(x * x, axis=-1, keepdims=True)
    xb_ref[...] = (x * g_ref[...]).astype(xb_ref.dtype)
    rs_ref[...] = jnp.broadcast_to(lax.rsqrt(ms + NORM_EPS), rs_ref.shape)


def gain_rstd(x, g, tm=512):
    S, D = x.shape
    tm = min(tm, S)
    vmem = 2 * (_nbytes((tm, D), F32) + _nbytes((tm, D), BF16)) + _nbytes((tm, D), F32)
    return pl.pallas_call(
        _gain_rstd_kernel,
        grid=(S // tm,),
        in_specs=[pl.BlockSpec((tm, D), lambda m: (m, 0)),
                  pl.BlockSpec((1, D), lambda m: (0, 0))],
        out_specs=[pl.BlockSpec((tm, D), lambda m: (m, 0)),
                   pl.BlockSpec((tm, LANES), lambda m: (m, 0))],
        out_shape=[jax.ShapeDtypeStruct((S, D), BF16), jax.ShapeDtypeStruct((S, LANES), F32)],
        compiler_params=_cparams(1, vmem),
        name="gain_rstd",
    )(x, g.reshape(1, D))


class _TiledParam:
    def __init__(self, refs):
        self.refs = refs

    def __getitem__(self, idx):
        parts = [r[idx] for r in self.refs]
        return parts[0] if len(parts) == 1 else jnp.concatenate(parts, axis=-1)


def _proj_kernel(*refs, w_src, w_transposed, n_blocks, n_p, tiles, mode, tm, tn, n_taps, row_block,
                 row_chunk, q_tiles, has_side):
    n_w = len(w_src)
    h_ref, rs_ref = refs[0], refs[1]
    w_hbm = refs[2:2 + n_w]
    p_all = refs[2 + n_w:2 + n_w + n_p * tiles]
    p_refs = [_TiledParam(p_all[k * tiles:(k + 1) * tiles]) for k in range(n_p)]
    rest = list(refs[2 + n_w + n_p * tiles:])
    side_in = rest.pop(0) if has_side else None
    o_ref = rest.pop(0)
    side_out = rest.pop(0) if has_side else None
    stage_ref, sem, wbf_ref = rest[:3]
    ext_ref = rest[3] if n_taps else None
    tb = tn // tiles
    n = pl.program_id(0)
    m = pl.program_id(1)

    def weight_copies(step):
        copies = []
        for i, (layer, first_block) in enumerate(w_src):
            for t in range(tiles):
                blk = jnp.minimum(step * tiles + t, n_blocks - 1) + first_block
                cols = pl.ds(pl.multiple_of(blk * tb, tb), tb)
                src = w_hbm[i].at[layer, cols, :] if w_transposed else w_hbm[i].at[layer, :, cols]
                k = i * tiles + t
                copies.append(pltpu.make_async_copy(src, stage_ref.at[k], sem.at[k]))
        return copies

    @pl.when(m == 0)
    def _():
        @pl.when(n == 0)
        def _():
            for c in weight_copies(n):
                c.start()

        for k, c in enumerate(weight_copies(n)):
            c.wait()
            if w_transposed:
                wbf_ref[k * tb:(k + 1) * tb, :] = stage_ref[k].astype(BF16)
            else:
                wbf_ref[:, k * tb:(k + 1) * tb] = stage_ref[k].astype(BF16)

        @pl.when(n + 1 < pl.num_programs(0))
        def _():
            for c in weight_copies(n + 1):
                c.start()

        if n_taps:
            ext_ref[0:SUBLANES, :] = jnp.zeros((SUBLANES, tn), F32)

    def run(n_tiles):
        wd = n_tiles * tb

        def mm(h, c0, c1):
            if w_transposed:
                return lax.dot_general(h, wbf_ref[c0:c1, :], (((1,), (1,)), ((), ())),
                                       preferred_element_type=F32)
            return jnp.dot(h, wbf_ref[:, c0:c1], preferred_element_type=F32)

        def dots(r0):
            h = h_ref[r0:r0 + row_block, :]
            rs = rs_ref[r0:r0 + row_block, :]
            rs = jnp.concatenate([rs] * (wd // LANES), axis=1)
            if n_tiles == tiles:
                acc = mm(h, 0, n_w * tn)
                return [acc[:, i * tn:(i + 1) * tn] * rs for i in range(n_w)]
            return [mm(h, i * tn, i * tn + wd) * rs for i in range(n_w)]

        def param(k, idx=slice(None)):
            return p_refs[k][idx, :][:, 0:wd]

        def taps(r):
            y = None
            for k in range(n_taps):
                off = SUBLANES - (n_taps - 1 - k) + r
                term = ext_ref[off:off + row_chunk, 0:wd] * param(0, slice(k, k + 1))
                y = term if y is None else y + term
            return y

        def to_ext(r0, val):
            ext_ref[SUBLANES + r0:SUBLANES + r0 + row_block, 0:wd] = val

        def chunks(r0):
            return [(r, r - r0) for r in range(r0, r0 + row_block, row_chunk)]

        if mode == "plain":
            def matmul(r0):
                return dots(r0)[0]

            def finish(r0, acc):
                o_ref[r0:r0 + row_block, 0:wd] = acc.astype(o_ref.dtype)
        elif mode == "glu_bias":
            def matmul(r0):
                return dots(r0)

            def finish(r0, accs):
                val = accs[0] + param(0)
                gate = accs[1] + param(1)
                o_ref[r0:r0 + row_block, 0:wd] = (val * jax.nn.sigmoid(gate)).astype(o_ref.dtype)
        elif mode == "conv_silu":
            def matmul(r0):
                to_ext(r0, dots(r0)[0])

            def finish(r0, _):
                for r, _ in chunks(r0):
                    o_ref[r:r + row_chunk, 0:wd] = _silu(taps(r)).astype(o_ref.dtype)
        elif mode == "conv_silu_l2":
            scale = jnp.where(n < q_tiles, GDN_HEAD_DIM ** -0.5, 1.0).astype(F32)

            def matmul(r0):
                to_ext(r0, dots(r0)[0])

            def finish(r0, _):
                for r, _ in chunks(r0):
                    y = _silu(taps(r))
                    for j in range(0, wd, GDN_HEAD_DIM):
                        yh = y[:, j:j + GDN_HEAD_DIM]
                        inv = lax.rsqrt(jnp.sum(yh * yh, axis=-1, keepdims=True) + L2_EPS)
                        o_ref[r:r + row_chunk, j:j + GDN_HEAD_DIM] = (yh * inv * scale).astype(o_ref.dtype)
        elif mode == "short_conv":
            def matmul(r0):
                bg, cg, xin = dots(r0)
                to_ext(r0, cg * xin)
                return bg

            def finish(r0, bg):
                for r, rr in chunks(r0):
                    o_ref[r:r + row_chunk, 0:wd] = (bg[rr:rr + row_chunk, :] * taps(r)).astype(o_ref.dtype)
        elif mode == "conv_glu":
            def matmul(r0):
                gate, up = dots(r0)
                to_ext(r0, gate)
                return up

            def finish(r0, up):
                bias = param(1)
                for r, rr in chunks(r0):
                    o_ref[r:r + row_chunk, 0:wd] = (_silu(taps(r) + bias) * up[rr:rr + row_chunk, :]).astype(o_ref.dtype)
        else:
            raise ValueError(mode)

        pending = None
        for r0 in range(0, tm, row_block):
            acc = matmul(r0)
            if pending is not None:
                finish(*pending)
            pending = (r0, acc)
        finish(*pending)
        if n_taps:
            ext_ref[0:SUBLANES, 0:wd] = ext_ref[tm:tm + SUBLANES, 0:wd]

    if n_blocks % tiles == 0:
        run(tiles)
    else:
        last = pl.num_programs(0) - 1

        @pl.when(n < last)
        def _():
            run(tiles)

        @pl.when(n == last)
        def _():
            run(n_blocks % tiles)

    if has_side:
        side_out[...] = side_in[...].astype(side_out.dtype)


def _side_rows(n_rows, n_steps):
    bf16_rows = 2 * SUBLANES
    for r in range(bf16_rows, n_rows + 1, bf16_rows):
        if n_rows % r == 0 and n_rows // r <= n_steps:
            return r
    raise ValueError((n_rows, n_steps))


def proj(hn, weights, params, mode, n_out, out_dtype, *, tn, tiles=1, tm=None, n_taps=0, q_tiles=0,
         row_block=512, row_chunk=128, side=None, w_transposed=False):
    h, rstd = hn
    S, K = h.shape
    if tm is None:
        tm = PROJ_ROWS_BF16_OUT if jnp.dtype(out_dtype).itemsize == 2 else PROJ_ROWS_F32_OUT
    tm = min(tm, S)
    row_block = min(row_block, tm)
    assert S % tm == 0 and n_out % tn == 0 and tm % row_block == 0 and row_block % row_chunk == 0
    n_w = len(weights)
    n_blocks = n_out // tn
    in_specs = [pl.BlockSpec((tm, K), lambda n, m: (m, 0)),
                pl.BlockSpec((tm, LANES), lambda n, m: (m, 0))]
    args = [h, rstd]
    vmem = 2 * (_nbytes((tm, K), h.dtype) + _nbytes((tm, LANES), F32))

    def col_block(n, t, o):
        return jnp.minimum(n * tiles + t, n_blocks - 1) + o

    w_src = []
    for w, layer, off in weights:
        assert w.shape[2 if w_transposed else 1] == K and off % tn == 0 and w.dtype == F32
        in_specs.append(pl.BlockSpec(memory_space=pl.ANY))
        args.append(w)
        w_src.append((layer, off // tn))
        vmem += tiles * (_nbytes((K, tn), F32) + _nbytes((K, tn), BF16))
    for p, off in params:
        for t in range(tiles):
            in_specs.append(pl.BlockSpec(
                (p.shape[0], tn), lambda n, m, t=t, o=off // tn: (0, col_block(n, t, o))))
            args.append(p)
        vmem += tiles * 2 * _nbytes((SUBLANES, tn), F32)
    tw = tiles * tn
    scratch = [pltpu.VMEM((n_w * tiles, tn, K) if w_transposed else (n_w * tiles, K, tn), F32),
               pltpu.SemaphoreType.DMA((n_w * tiles,)),
               pltpu.VMEM((n_w * tw, K) if w_transposed else (K, n_w * tw), BF16)]
    if n_taps:
        scratch.append(pltpu.VMEM((tm + SUBLANES, tw), F32))
        vmem += _nbytes((tm + SUBLANES, tw), F32)
    vmem += 2 * _nbytes((tm, tw), out_dtype) + n_w * _nbytes((row_block, tw), F32)
    n_steps_n, n_steps_m = pl.cdiv(n_blocks, tiles), S // tm
    out_specs = [pl.BlockSpec((tm, tw), lambda n, m: (m, n))]
    out_shape = [jax.ShapeDtypeStruct((S, n_out), out_dtype)]
    if side is not None:
        w_side, side_layer = side
        _, ks, ns = w_side.shape
        rows = _side_rows(ks, n_steps_n * n_steps_m)

        def side_block(n, m):
            return jnp.minimum(n * n_steps_m + m, ks // rows - 1)

        in_specs.append(pl.BlockSpec((None, rows, ns), lambda n, m: (side_layer, side_block(n, m), 0)))
        args.append(w_side)
        out_specs.append(pl.BlockSpec((rows, ns), lambda n, m: (side_block(n, m), 0)))
        out_shape.append(jax.ShapeDtypeStruct((ks, ns), BF16))
        vmem += 2 * (_nbytes((rows, ns), F32) + _nbytes((rows, ns), BF16))
    kern = functools.partial(_proj_kernel, w_src=tuple(w_src), w_transposed=w_transposed,
                             n_blocks=n_blocks, n_p=len(params),
                             tiles=tiles, mode=mode, tm=tm, tn=tw, n_taps=n_taps,
                             row_block=row_block, row_chunk=row_chunk, q_tiles=q_tiles,
                             has_side=side is not None)
    outs = pl.pallas_call(
        kern,
        grid=(n_steps_n, n_steps_m),
        in_specs=in_specs,
        out_specs=out_specs,
        out_shape=out_shape,
        scratch_shapes=scratch,
        compiler_params=_cparams(2, vmem),
        name="proj_" + mode,
    )(*args)
    return outs[0] if side is None else tuple(outs)


def _down_kernel(*refs, has_bias, n_total):
    a_ref, w_ref, x_ref, g_ref = refs[:4]
    b_ref = refs[4] if has_bias else None
    o_ref, xb_ref, rs_ref = refs[-3:]
    n = pl.program_id(1)

    @pl.when(n == 0)
    def _():
        rs_ref[...] = jnp.zeros(rs_ref.shape, F32)

    acc = jnp.dot(a_ref[...], w_ref[...], preferred_element_type=F32)
    if has_bias:
        acc = acc + b_ref[...]
    xn = x_ref[...] + acc
    o_ref[...] = xn
    xb_ref[...] = (xn * g_ref[...]).astype(xb_ref.dtype)
    sq = xn * xn
    part = sq[:, 0:LANES]
    for j in range(LANES, sq.shape[1], LANES):
        part = part + sq[:, j:j + LANES]
    rs_ref[...] = rs_ref[...] + part

    @pl.when(n == pl.num_programs(1) - 1)
    def _():
        ms = jnp.sum(rs_ref[...], axis=-1, keepdims=True) / n_total
        rs_ref[...] = jnp.broadcast_to(lax.rsqrt(ms + NORM_EPS), rs_ref.shape)


def down_residual(a, w_bf16, layer, x, gain, bias=None, *, tm=512, tn=512):
    S, K = a.shape
    N = w_bf16.shape[2]
    tm = min(tm, S)
    vmem =2 * (_nbytes((tm, K), a.dtype) + _nbytes((K, tn), BF16) + 2 * _nbytes((tm, tn), F32)
                + _nbytes((tm, tn), BF16) + _nbytes((tm, LANES), F32))
    vmem += 2 * _nbytes((tm, tn), F32)
    in_specs = [pl.BlockSpec((tm, K), lambda m, n: (m, 0)),
                pl.BlockSpec((None, K, tn), lambda m, n: (layer, 0, n)),
                pl.BlockSpec((tm, tn), lambda m, n: (m, n)),
                pl.BlockSpec((1, tn), lambda m, n: (0, n))]
    args = [a, w_bf16, x, gain.reshape(1, N)]
    if bias is not None:
        in_specs.append(pl.BlockSpec((1, tn), lambda m, n: (0, n)))
        args.append(bias.reshape(1, N))
    x_new, xb, rstd = pl.pallas_call(
        functools.partial(_down_kernel, has_bias=bias is not None, n_total=N),
        grid=(S // tm, N // tn),
        in_specs=in_specs,
        out_specs=[pl.BlockSpec((tm, tn), lambda m, n: (m, n)),
                   pl.BlockSpec((tm, tn), lambda m, n: (m, n)),
                   pl.BlockSpec((tm, LANES), lambda m, n: (m, 0))],
        out_shape=[jax.ShapeDtypeStruct((S, N), F32), jax.ShapeDtypeStruct((S, N), BF16),
                   jax.ShapeDtypeStruct((S, LANES), F32)],
        compiler_params=_cparams(2, vmem),
        name="down_residual",
    )(*args)
    return x_new, (xb, rstd)


def _swa_kernel(sink_ref, q_ref, k_ref, v_ref, cos_ref, sin_ref, o_ref, kprev_ref, vprev_ref,
                *, n_kv, kv_group):
    b = pl.program_id(0)
    D = ATT_HEAD_DIM
    blk = q_ref.shape[0]

    @pl.when(b == 0)
    def _():
        kprev_ref[...] = jnp.zeros(kprev_ref.shape, F32)
        vprev_ref[...] = jnp.zeros(vprev_ref.shape, F32)

    cos = cos_ref[...]
    sin = sin_ref[...]

    def rope(x):
        return x * cos + pltpu.roll(x, D // 2, 1) * sin

    qi = lax.broadcasted_iota(jnp.int32, (blk, 2 * blk), 0)
    kj = lax.broadcasted_iota(jnp.int32, (blk, 2 * blk), 1)
    rel = blk + qi - kj
    valid = (rel >= 0) & (rel < WINDOW) & ((kj >= blk) | (b > 0))
    scale = D ** -0.5

    for g0 in range(0, n_kv, kv_group):
        gs = range(g0, g0 + kv_group)
        kband, vband, s, p = {}, {}, {}, {}
        for g in gs:
            kc = rope(k_ref[:, g * D:(g + 1) * D])
            vc = v_ref[:, g * D:(g + 1) * D]
            kband[g] = jnp.concatenate([kprev_ref[:, g * D:(g + 1) * D], kc], axis=0).astype(BF16)
            vband[g] = jnp.concatenate([vprev_ref[:, g * D:(g + 1) * D], vc], axis=0).astype(BF16)
            kprev_ref[:, g * D:(g + 1) * D] = kc
            vprev_ref[:, g * D:(g + 1) * D] = vc
        hs = [(g, g * ATT_GROUP + gg) for g in gs for gg in range(ATT_GROUP)]
        for g, hh in hs:
            qh = rope(q_ref[:, hh * D:(hh + 1) * D])
            s[hh] = _mm_nt(qh, kband[g]) * scale
        for g, hh in hs:
            sh = jnp.where(valid, s[hh], -jnp.inf)
            sink = sink_ref[hh]
            mx = jnp.maximum(jnp.max(sh, axis=-1, keepdims=True), sink)
            e = jnp.exp(sh - mx)
            denom = jnp.sum(e, axis=-1, keepdims=True) + jnp.exp(sink - mx)
            p[hh] = e / denom
        for g, hh in hs:
            o_ref[:, hh * D:(hh + 1) * D] = _mm(p[hh], vband[g]).astype(o_ref.dtype)


def swa_attention(qkv, sinks, cos_full, sin_signed, n_heads, n_kv):
    S = qkv.shape[0]
    D = ATT_HEAD_DIM
    blk = WINDOW
    qw, kw = n_heads * D, n_kv * D
    vmem = 2 * (_nbytes((blk, qw), F32) + 2 * _nbytes((blk, kw), F32) + 2 * _nbytes((blk, D), F32)
                + _nbytes((blk, qw), BF16)) + 2 * _nbytes((blk, kw), F32)
    return pl.pallas_call(
        functools.partial(_swa_kernel, n_kv=n_kv, kv_group=math.gcd(n_kv, ATT_KV_GROUP)),
        grid=(S // blk,),
        in_specs=[pl.BlockSpec(memory_space=pltpu.SMEM),
                  pl.BlockSpec((blk, qw), lambda b: (b, 0)),
                  pl.BlockSpec((blk, kw), lambda b: (b, qw // kw)),
                  pl.BlockSpec((blk, kw), lambda b: (b, qw // kw + 1)),
                  pl.BlockSpec((blk, D), lambda b: (b, 0)),
                  pl.BlockSpec((blk, D), lambda b: (b, 0))],
        out_specs=pl.BlockSpec((blk, qw), lambda b: (b, 0)),
        out_shape=jax.ShapeDtypeStruct((S, qw), BF16),
        scratch_shapes=[pltpu.VMEM((blk, kw), F32), pltpu.VMEM((blk, kw), F32)],
        compiler_params=_cparams(1, vmem),
        name="swa_attention",
    )(sinks, qkv, qkv, qkv, cos_full, sin_signed)


def _delta_kernel(q_ref, k_ref, v_ref, z_ref, ba_ref, prm_ref, nw_ref, o_ref, st_ref,
                  *, n_v, rep, head_group):
    c = pl.program_id(0)
    C = q_ref.shape[0]
    D = GDN_HEAD_DIM

    @pl.when(c == 0)
    def _():
        st_ref[...] = jnp.zeros(st_ref.shape, F32)

    ba = ba_ref[...]
    beta_t = jax.nn.sigmoid(ba)
    a_sh = ba + prm_ref[1:2, :]
    softplus = jnp.maximum(a_sh, 0.0) + jnp.log(1.0 + jnp.exp(-jnp.abs(a_sh)))
    g_t = -jnp.exp(prm_ref[0:1, :]) * softplus
    row = lax.broadcasted_iota(jnp.int32, (C, LANES), 0)
    gc = g_t
    s = 1
    while s < C:
        gc = gc + jnp.where(row >= s, pltpu.roll(gc, s, 0), 0.0)
        s *= 2
    glast = gc[C - 1:C, :]
    e_gc = jnp.exp(gc)
    e_rest = jnp.exp(glast - gc)
    e_last = jnp.exp(glast)
    gc_t = gc.T

    ii = lax.broadcasted_iota(jnp.int32, (C, C), 0)
    jj = lax.broadcasted_iota(jnp.int32, (C, C), 1)
    lower = ii >= jj
    strict = ii > jj
    same_blk = (ii // GDN_SOLVE_BLOCK) == (jj // GDN_SOLVE_BLOCK)
    nw = nw_ref[...]

    for g0 in range(0, n_v, head_group):
        hs = list(range(g0, g0 + head_group))
        js = sorted({i // rep for i in hs})
        kh = {j: k_ref[:, j * D:(j + 1) * D] for j in js}
        qh = {j: q_ref[:, j * D:(j + 1) * D] for j in js}
        kk = {j: _mm_nt(kh[j], kh[j]) for j in js}
        qk = {j: _mm_nt(qh[j], kh[j]) for j in js}
        st = {i: st_ref[i] for i in hs}
        qg, attn, a_o, tp, pw, rhs = {}, {}, {}, {}, {}, {}
        for i in hs:
            j, gi = i // rep, n_v + i
            bcol = beta_t[:, i:i + 1]
            dec = jnp.exp(jnp.where(lower, gc[:, gi:gi + 1] - gc_t[gi:gi + 1, :], -jnp.inf))
            a_kk = jnp.where(strict, bcol * kk[j] * dec, 0.0)
            attn[i] = qk[j] * dec
            qg[i] = qh[j] * e_gc[:, gi:gi + 1]
            a_d = jnp.where(same_blk, a_kk, 0.0)
            a_o[i] = a_kk - a_d
            tp[i] = -a_d
            pw[i] = tp[i]
            kb = kh[j] * bcol
            rhs[i] = jnp.concatenate([v_ref[:, i * D:(i + 1) * D] * bcol, kb * e_gc[:, gi:gi + 1]],
                                     axis=1)
        n_sq = int(math.log2(GDN_SOLVE_BLOCK)) - 1
        for i in hs:
            pw[i] = _mm(pw[i], pw[i])
        for s in range(n_sq):
            for i in hs:
                tp[i] = tp[i] + pw[i] + _mm(tp[i], pw[i])
            if s + 1 < n_sq:
                for i in hs:
                    pw[i] = _mm(pw[i], pw[i])
        og = {i: _mm(qg[i], st[i]) for i in hs}
        nn = {i: a_o[i] + _mm(tp[i], a_o[i]) for i in hs}
        y = {i: rhs[i] + _mm(tp[i], rhs[i]) for i in hs}
        n2 = {i: _mm(nn[i], nn[i]) for i in hs}
        dd = {i: y[i] - _mm(nn[i], y[i]) for i in hs}
        x = {i: dd[i] + _mm(n2[i], dd[i]) for i in hs}
        v_new = {i: x[i][:, :D] - _mm(x[i][:, D:], st[i]) for i in hs}
        for i in hs:
            j, gi = i // rep, n_v + i
            out = og[i] + _mm(attn[i], v_new[i])
            kd = kh[j] * e_rest[:, gi:gi + 1]
            st_ref[i] = st[i] * e_last[:, gi:gi + 1] + _mm(kd.T, v_new[i])
            ms = jnp.mean(out * out, axis=-1, keepdims=True)
            zh = z_ref[:, i * D:(i + 1) * D]
            o_ref[:, i * D:(i + 1) * D] = (out * lax.rsqrt(ms + NORM_EPS) * nw * _silu(zh)).astype(o_ref.dtype)


def gated_delta(qk, v, z, ba, prm, norm_w, n_qk, n_v):
    S = qk.shape[0]
    D = GDN_HEAD_DIM
    C = GDN_CHUNK
    qw, vw = n_qk * D, n_v * D
    assert GDN_SOLVE_BLOCK * 4 == C
    vmem = 2 * (2 * _nbytes((C, qw), F32) + 2 * _nbytes((C, vw), F32) + _nbytes((C, vw), BF16))
    vmem += _nbytes((n_v, D, D), F32)
    return pl.pallas_call(
        functools.partial(_delta_kernel, n_v=n_v, rep=n_v // n_qk,
                          head_group=math.gcd(n_v, GDN_HEAD_GROUP)),
        grid=(S // C,),
        in_specs=[pl.BlockSpec((C, qw), lambda c: (c, 0)),
                  pl.BlockSpec((C, qw), lambda c: (c, 1)),
                  pl.BlockSpec((C, vw), lambda c: (c, 0)),
                  pl.BlockSpec((C, vw), lambda c: (c, 0)),
                  pl.BlockSpec((C, LANES), lambda c: (c, 0)),
                  pl.BlockSpec((2, LANES), lambda c: (0, 0)),
                  pl.BlockSpec((1, D), lambda c: (0, 0))],
        out_specs=pl.BlockSpec((C, vw), lambda c: (c, 0)),
        out_shape=jax.ShapeDtypeStruct((S, vw), BF16),
        scratch_shapes=[pltpu.VMEM((n_v, D, D), F32)],
        compiler_params=_cparams(1, vmem),
        name="gated_delta",
    )(qk, qk, v, z, ba, prm, norm_w.reshape(1, D))


CONF_HIST = 32


def _conf_kernel(u_ref, w_ref, bdw_ref, g_ref, b_ref, o_ref, ext_ref, sh_ref, y_ref, *, tm, cw,
                 n_c, row_chunk):
    m = pl.program_id(0)
    c = pl.program_id(1)

    @pl.when(m == 0)
    def _():
        ext_ref[c, 0:CONF_HIST, :] = jnp.zeros((CONF_HIST, cw), F32)

    ext_ref[c, CONF_HIST:CONF_HIST + tm, :] = u_ref[...]
    n_rows = CONF_HIST + tm - SUBLANES
    for j in range(1, SUBLANES):
        sh_ref[j - 1, SUBLANES:SUBLANES + n_rows, :] = ext_ref[c, SUBLANES - j:SUBLANES - j + n_rows, :]
    bias = bdw_ref[...]
    for r in range(0, tm, row_chunk):
        y = None
        for k in range(CONF_KERNEL):
            shift = CONF_KERNEL - 1 - k
            a, j = divmod(shift, SUBLANES)
            off = CONF_HIST - SUBLANES * a + r
            if j == 0:
                src = ext_ref[c, off:off + row_chunk, :]
            else:
                src = sh_ref[j - 1, off:off + row_chunk, :]
            term = src * w_ref[k:k + 1, :]
            y = term if y is None else y + term
        y_ref[c, r:r + row_chunk, :] = y + bias
    ext_ref[c, 0:CONF_HIST, :] = ext_ref[c, tm:tm + CONF_HIST, :]

    @pl.when(c == n_c - 1)
    def _():
        d = n_c * cw
        tot = jnp.zeros((tm, 1), F32)
        for j in range(n_c):
            tot = tot + jnp.sum(y_ref[j], axis=-1, keepdims=True)
        mean = tot / d
        var = jnp.zeros((tm, 1), F32)
        for j in range(n_c):
            yc = y_ref[j] - mean
            var = var + jnp.sum(yc * yc, axis=-1, keepdims=True)
        inv = lax.rsqrt(var / d + LN_EPS)
        for j in range(n_c):
            zz = (y_ref[j] - mean) * inv * g_ref[:, j * cw:(j + 1) * cw] + b_ref[:, j * cw:(j + 1) * cw]
            o_ref[:, j * cw:(j + 1) * cw] = _silu(zz).astype(o_ref.dtype)


def conformer_mid(u, w_dw, b_dw, ln_g, ln_b, *, tm=512, cw=512, row_chunk=32):
    S, D = u.shape
    tm, cw = min(tm, S), min(cw, D)
    n_c = D // cw
    vmem = 2 * (_nbytes((tm, cw), F32) + _nbytes((CONF_HIST, cw), F32) + _nbytes((tm, D), BF16))
    vmem += _nbytes((n_c + SUBLANES - 1, tm + CONF_HIST, cw), F32) + 3 * _nbytes((n_c, tm, cw), F32)
    kern = functools.partial(_conf_kernel, tm=tm, cw=cw, n_c=n_c, row_chunk=row_chunk)
    return pl.pallas_call(
        kern,
        grid=(S // tm, n_c),
        in_specs=[pl.BlockSpec((tm, cw), lambda m, c: (m, c)),
                  pl.BlockSpec((CONF_KERNEL, cw), lambda m, c: (0, c)),
                  pl.BlockSpec((1, cw), lambda m, c: (0, c)),
                  pl.BlockSpec((1, D), lambda m, c: (0, 0)),
                  pl.BlockSpec((1, D), lambda m, c: (0, 0))],
        out_specs=pl.BlockSpec((tm, D), lambda m, c: (m, 0)),
        out_shape=jax.ShapeDtypeStruct((S, D), BF16),
        scratch_shapes=[pltpu.VMEM((n_c, tm + CONF_HIST, cw), F32),
                        pltpu.VMEM((SUBLANES - 1, tm + CONF_HIST, cw), F32),
                        pltpu.VMEM((n_c, tm, cw), F32)],
        compiler_params=_cparams(2, vmem),
        name="conformer_mid",
    )(u, w_dw, b_dw.reshape(1, D), ln_g.reshape(1, D), ln_b.reshape(1, D))


def _rope_tables(S):
    half = ATT_HEAD_DIM // 2
    inv = jnp.power(ROPE_THETA, -jnp.arange(half, dtype=F32) / half)
    ang = jnp.arange(S).astype(F32)[:, None] * inv[None, :]
    cos, sin = jnp.cos(ang), jnp.sin(ang)
    return jnp.concatenate([cos, cos], axis=1), jnp.concatenate([-sin, sin], axis=1)


def _wide_tile(n, cap=512):
    t = cap
    while n % t:
        t //= 2
    assert t >= LANES
    return t


def _out_proj(a, w_o_bf16, x, gain, bias=None):
    return down_residual(a, w_o_bf16[None], 0, x, gain, bias, tm=1024, tn=_wide_tile(x.shape[1]))


def _mixer_a(hn, x, gain, w_qkv, w_o, sinks, j):
    S, Dm = x.shape
    n_heads = w_o.shape[1] // ATT_HEAD_DIM
    n_kv = n_heads // ATT_GROUP
    n_qkv = w_qkv.shape[2]
    qkv, w_o_bf16 = proj(hn, [(w_qkv, j, 0)], [], "plain", n_qkv, F32,
                         tn=_wide_tile(n_qkv, PROJ_COLS), side=(w_o, j))
    cos_full, sin_signed = _rope_tables(S)
    o = swa_attention(qkv, sinks[j], cos_full, sin_signed, n_heads, n_kv)
    return _out_proj(o, w_o_bf16, x, gain)


def _mixer_b(hn, x, gain, w_in, conv_w, a_log, dt_bias, norm_w, w_o, j):
    S, Dm = x.shape
    D = GDN_HEAD_DIM
    conv_w, a_log, dt_bias, norm_w = conv_w[j], a_log[j], dt_bias[j], norm_w[j]
    n_v = a_log.shape[0]
    val_dim = n_v * D
    key_dim = (conv_w.shape[1] - val_dim) // 2
    n_qk = key_dim // D
    n_qkv = 2 * key_dim + val_dim
    tn = _wide_tile(key_dim, PROJ_COLS)
    n_taps = conv_w.shape[0]
    w_in_t = jnp.swapaxes(w_in, 1, 2)
    qk, w_o_bf16 = proj(hn, [(w_in_t, j, 0)], [(conv_w, 0)], "conv_silu_l2", 2 * key_dim, F32, tn=tn,
                        n_taps=n_taps, q_tiles=key_dim // tn, side=(w_o, j), w_transposed=True,
                        row_block=256)
    v = proj(hn, [(w_in_t, j, 2 * key_dim)], [(conv_w, 2 * key_dim)], "conv_silu", val_dim, F32,
             tn=tn, n_taps=n_taps, w_transposed=True)
    z = proj(hn, [(w_in_t, j, n_qkv)], [], "plain", val_dim, F32, tn=tn, w_transposed=True)
    w_ba_t = jnp.pad(w_in_t[j][n_qkv + val_dim:, :], ((0, LANES - 2 * n_v), (0, 0)))
    ba = proj(hn, [(w_ba_t[None], 0, 0)], [], "plain", LANES, F32, tn=LANES, w_transposed=True)
    prm = jnp.zeros((2, LANES), F32)
    prm = prm.at[0, n_v:2 * n_v].set(a_log).at[1, n_v:2 * n_v].set(dt_bias)
    o = gated_delta(qk, v, z, ba, prm, norm_w, n_qk, n_v)
    return _out_proj(o, w_o_bf16, x, gain)


def _mixer_c(hn, x, gain, w_pw1, b_pw1, w_dw, b_dw, ln_g, ln_b, w_pw2, b_pw2, j):
    S, Dm = x.shape
    b1 = b_pw1[j].reshape(1, -1)
    u, w_pw2_bf16 = proj(hn, [(w_pw1, j, 0), (w_pw1, j, Dm)], [(b1, 0), (b1, Dm)], "glu_bias", Dm,
                         F32, tn=_wide_tile(Dm, PROJ_COLS // 2), side=(w_pw2, j))
    mid = conformer_mid(u, w_dw[j], b_dw[j], ln_g[j], ln_b[j])
    return _out_proj(mid, w_pw2_bf16, x, gain, b_pw2[j])


def _mixer_d(hn, x, gain, w_in, w_conv, w_out, j):
    S, Dm = x.shape
    mid, w_out_bf16 = proj(hn, [(w_in, j, 0), (w_in, j, Dm), (w_in, j, 2 * Dm)], [(w_conv[j], 0)],
                           "short_conv", Dm, BF16, tn=MXU_COLS, n_taps=w_conv.shape[1],
                           side=(w_out, j))
    return _out_proj(mid, w_out_bf16, x, gain)


def _ffn(hn, x, gain, w_gate, w_up, w_conv, b_conv, w_down, i):
    d_ff = w_gate.shape[2]
    mid, w_down_bf16 = proj(hn, [(w_gate, i, 0), (w_up, i, 0)],
                            [(w_conv[i], 0), (b_conv[i].reshape(1, -1), 0)], "conv_glu", d_ff, BF16,
                            tn=MXU_COLS, tiles=PROJ_COLS // (2 * MXU_COLS), n_taps=w_conv.shape[1],
                            side=(w_down, i))
    return down_residual(mid, w_down_bf16[None], 0, x, gain)


def kernel(x, mix_norm, ffn_norm, final_norm, a_w_qkv, a_w_o, a_sinks, b_w_in, b_conv, b_a_log, b_dt_bias, b_norm, b_w_o, c_w_pw1, c_b_pw1, c_w_dw, c_b_dw, c_ln_g, c_ln_b, c_w_pw2, c_b_pw2, d_w_in, d_w_conv, d_w_out, f_w_gate, f_w_up, f_w_conv, f_b_conv, f_w_down):
    B, S, Dm = x.shape
    depth = mix_norm.shape[0]
    outs = []
    for bi in range(B):
        xb = x[bi]
        hn = gain_rstd(xb, mix_norm[0])
        for i in range(depth):
            kind, j = i % 4, i // 4
            g_ffn = ffn_norm[i]
            if kind == 0:
                xb, hn = _mixer_a(hn, xb, g_ffn, a_w_qkv, a_w_o, a_sinks, j)
            elif kind == 1:
                xb, hn = _mixer_b(hn, xb, g_ffn, b_w_in, b_conv, b_a_log, b_dt_bias, b_norm, b_w_o, j)
            elif kind == 2:
                xb, hn = _mixer_c(hn, xb, g_ffn, c_w_pw1, c_b_pw1, c_w_dw, c_b_dw, c_ln_g, c_ln_b,
                                  c_w_pw2, c_b_pw2, j)
            else:
                xb, hn = _mixer_d(hn, xb, g_ffn, d_w_in, d_w_conv, d_w_out, j)
            g_next = mix_norm[i + 1] if i + 1 < depth else final_norm
            xb, hn = _ffn(hn, xb, g_next, f_w_gate, f_w_up, f_w_conv, f_b_conv, f_w_down, i)
        outs.append(rmsnorm(xb, final_norm, F32))
    return jnp.stack(outs, axis=0)
```
